```python
import math
import jax
import jax.numpy as jnp
from jax import lax
import numpy as np

D_MODEL = 1024
BATCH = 8
SEQ = 4096
DEPTH = 2

GRID_W = 64
CTX_LEN = 256
EPS = 1e-6
W_LRU = 256
LRU_BLOCKS = 4
LRU_BLK = W_LRU // LRU_BLOCKS
CONV_W = 4
LRU_C = 8.0
W_S5 = 256
S5_H = 16
S5_G = W_S5 // S5_H
S5_P = 64
DA_HEADS = 4
DA_DH = 64
DA_DV = 2 * DA_DH
W_DA = DA_HEADS * DA_DV
MIX_W = W_LRU + W_S5 + W_DA
SIDE_W = W_LRU + W_S5 + 2 * W_DA
IN_W = 2 * SIDE_W
SPLIT4 = (W_LRU, W_LRU + W_S5, W_LRU + W_S5 + W_DA)
Q_BLOCK = 128
ROPE_BASE = 10000.0
ROPE_F = DA_DH // 4

kernel_name = 'hybrid_lru_s5_diffattn_prefix'

F32 = jnp.float32


def rms_norm(x, g):
    xf = x.astype(F32)
    y = xf * lax.rsqrt(jnp.mean(xf * xf, axis=-1, keepdims=True) + EPS)
    return (y * g.astype(F32)).astype(x.dtype)


def centred_dwconv(u, w, b):
    left = CONV_W // 2
    y = lax.conv_general_dilated(u, w.astype(u.dtype)[:, None, :], window_strides=(1,),
                                 padding=[(left, CONV_W - 1 - left)],
                                 dimension_numbers=('NWC', 'WIO', 'NWC'),
                                 feature_group_count=u.shape[-1])
    return y + b.astype(u.dtype)


def real_linear_op(e1, e2):
    a1, b1 = e1
    a2, b2 = e2
    return a1 * a2, a2 * b1 + b2


def complex_linear_op(e1, e2):
    ar1, ai1, br1, bi1 = e1
    ar2, ai2, br2, bi2 = e2
    return (ar1 * ar2 - ai1 * ai2, ar1 * ai2 + ai1 * ar2,
            ar2 * br1 - ai2 * bi1 + br2, ar2 * bi1 + ai2 * br1 + bi2)


def real_scan(a, b, h0, reverse):
    if h0 is not None:
        idx = a.shape[1] - 1 if reverse else 0
        b = b.at[:, idx].add(a[:, idx] * h0)
    _, h = lax.associative_scan(real_linear_op, (a, b), reverse=reverse, axis=1)
    return h


def block_diag(xb, w):
    y = jnp.einsum('btnc,ncd->btnd', xb, w.astype(F32))
    return y.reshape(xb.shape[0], xb.shape[1], -1)


def rglru_scans(u, conv_w, conv_b, wa, ba, wx, bx, lam, h0):
    bsz, t, _ = u.shape
    xc = centred_dwconv(u, conv_w, conv_b).astype(F32)
    xb = xc.reshape(bsz, t, LRU_BLOCKS, LRU_BLK)
    hs = []
    for d in range(2):
        gate_r = jax.nn.sigmoid(block_diag(xb, wa[d]) + ba[d].astype(F32))
        gate_i = jax.nn.sigmoid(block_diag(xb, wx[d]) + bx[d].astype(F32))
        log_a = -LRU_C * gate_r * jax.nn.softplus(-lam[d].astype(F32))
        a = jnp.exp(log_a)
        b = jnp.sqrt(-jnp.expm1(2.0 * log_a)) * (gate_i * xc)
        hs.append(real_scan(a, b, None if h0 is None else h0[d], reverse=(d == 1)))
    return hs


def s5_discretise(lam_re, lam_im, log_dt, b_re, b_im):
    lam_re = lam_re.astype(F32)
    lam_im = lam_im.astype(F32)
    dt = jnp.exp(log_dt.astype(F32))[:, None]
    mag = jnp.exp(lam_re * dt)
    ang = lam_im * dt
    ab_re = mag * jnp.cos(ang)
    ab_im = mag * jnp.sin(ang)
    nr = ab_re - 1.0
    ni = ab_im
    den = lam_re * lam_re + lam_im * lam_im
    cf_re = ((nr * lam_re + ni * lam_im) / den)[..., None]
    cf_im = ((ni * lam_re - nr * lam_im) / den)[..., None]
    b_re = b_re.astype(F32)
    b_im = b_im.astype(F32)
    return ab_re, ab_im, cf_re * b_re - cf_im * b_im, cf_re * b_im + cf_im * b_re


def s5_scan(u, disc, h0, reverse):
    ab_re, ab_im, bb_re, bb_im = disc
    bsz, t, _ = u.shape
    ug = u.astype(F32).reshape(bsz, t, S5_G, S5_H)
    bu_re = jnp.einsum('btgh,gph->btgp', ug, bb_re)
    bu_im = jnp.einsum('btgh,gph->btgp', ug, bb_im)
    if h0 is not None:
        idx = t - 1 if reverse else 0
        h_re, h_im = h0
        bu_re = bu_re.at[:, idx].add(ab_re * h_re - ab_im * h_im)
        bu_im = bu_im.at[:, idx].add(ab_re * h_im + ab_im * h_re)
    a_re = jnp.broadcast_to(ab_re, bu_re.shape)
    a_im = jnp.broadcast_to(ab_im, bu_im.shape)
    _, _, s_re, s_im = lax.associative_scan(complex_linear_op, (a_re, a_im, bu_re, bu_im),
                                            reverse=reverse, axis=1)
    return s_re, s_im


def final_state(states, reverse):
    idx = 0 if reverse else -1
    return states[0][:, idx], states[1][:, idx]


def s5_output(u, states, c_re, c_im, d_skip, w_glu, b_glu):
    uf = u.astype(F32)
    bsz, t, _ = u.shape
    y = d_skip.astype(F32) * uf
    for d in range(2):
        s_re, s_im = states[d]
        r = (jnp.einsum('btgp,ghp->btgh', s_re, c_re[d].astype(F32))
             - jnp.einsum('btgp,ghp->btgh', s_im, c_im[d].astype(F32)))
        y = y + r.reshape(bsz, t, W_S5)
    z = jax.nn.gelu(y)
    return z * jax.nn.sigmoid(z @ w_glu.astype(F32) + b_glu.astype(F32))


def rot_half(x, ang):
    f = ang.shape[-1]
    cos = jnp.cos(ang).astype(x.dtype)[:, None, None, :]
    sin = jnp.sin(ang).astype(x.dtype)[:, None, None, :]
    x1, x2 = x[..., :f], x[..., f:]
    return jnp.concatenate([x1 * cos - x2 * sin, x2 * cos + x1 * sin], axis=-1)


def rope_axial(x, ang_row, ang_col):
    half = x.shape[-1] // 2
    return jnp.concatenate([rot_half(x[..., :half], ang_row), rot_half(x[..., half:], ang_col)], axis=-1)


def diff_softmax_attend(q, k, v, lam):
    s = jnp.einsum('bqhmd,bkhmd->bhmqk', q, k).astype(F32) * (DA_DH ** -0.5)
    p = jax.nn.softmax(s, axis=-1)
    w = p[:, :, 0] - lam * p[:, :, 1]
    return jnp.einsum('bhqk,bkhe->bqhe', w.astype(v.dtype), v)


def latent_diff_attention(q, k_all, v_all, lam):
    bsz, t = q.shape[:2]
    nb = t // Q_BLOCK
    qb = jnp.moveaxis(q.reshape(bsz, nb, Q_BLOCK, DA_HEADS, 2, DA_DH), 1, 0)
    ob = lax.map(lambda qq: diff_softmax_attend(qq, k_all, v_all, lam), qb)
    return jnp.moveaxis(ob, 0, 1).reshape(bsz, t, DA_HEADS, DA_DV)


def combine_branches(h_lru, s5_st, u_s5, o_att, g_lru, g_s5, g_att,
                     c_re, c_im, d_skip, w_glu, b_glu, da_g, lam_init, w_out):
    dtype = g_lru.dtype
    bsz, t, _ = g_lru.shape
    y_a = (h_lru[0] + h_lru[1]) * jax.nn.silu(g_lru.astype(F32))
    y_s = s5_output(u_s5, s5_st, c_re, c_im, d_skip, w_glu, b_glu) * jax.nn.silu(g_s5.astype(F32))
    o = rms_norm(o_att, da_g).astype(F32) * (1.0 - lam_init)
    y_d = o.reshape(bsz, t, W_DA) * jax.nn.silu(g_att.astype(F32))
    y = jnp.concatenate([y_a.astype(dtype), y_s.astype(dtype), y_d.astype(dtype)], axis=-1)
    return y @ w_out


def setup_inputs(seed: int = 0) -> dict:
    key = jax.random.key(seed)
    ks = iter(jax.random.split(key, 32))
    L, D = DEPTH, D_MODEL

    def nrm(shape, scale):
        return scale * jax.random.normal(next(ks), shape, F32)

    x = nrm((BATCH, SEQ, D), 1.0)
    c = nrm((BATCH, D), 1.0)
    ctx = nrm((BATCH, CTX_LEN, D), 1.0)
    c_ctx = nrm((D,), 1.0)
    norm_g = 1.0 + nrm((L, D), 0.02)
    w_mod = nrm((L, D, 3 * D), 0.5 * D ** -0.5)
    b_mod = nrm((L, 3 * D), 0.02)
    w_in = nrm((L, D, IN_W), D ** -0.5)
    w_out = nrm((L, MIX_W, D), MIX_W ** -0.5)
    lru_conv_w = nrm((L, CONV_W, W_LRU), CONV_W ** -0.5)
    lru_conv_b = nrm((L, W_LRU), 0.02)
    lru_wa = nrm((L, 2, LRU_BLOCKS, LRU_BLK, LRU_BLK), LRU_BLK ** -0.5)
    lru_ba = nrm((L, 2, W_LRU), 0.02)
    lru_wx = nrm((L, 2, LRU_BLOCKS, LRU_BLK, LRU_BLK), LRU_BLK ** -0.5)
    lru_bx = nrm((L, 2, W_LRU), 0.02)
    a_pow = jax.random.uniform(next(ks), (L, 2, W_LRU), F32, 0.9, 0.999)
    a0 = a_pow ** (1.0 / LRU_C)
    lru_lam = jnp.log(a0) - jnp.log1p(-a0)
    s5_lam_re = -0.5 + nrm((L, 2, S5_G, S5_P), 0.01)
    s5_lam_im = jnp.pi * jnp.arange(S5_P, dtype=F32) + nrm((L, 2, S5_G, S5_P), 0.01)
    s5_log_dt = jax.random.uniform(next(ks), (L, 2, S5_G), F32, math.log(1e-3), math.log(1e-1))
    s5_b_re = nrm((L, 2, S5_G, S5_P, S5_H), (2 * S5_H) ** -0.5)
    s5_b_im = nrm((L, 2, S5_G, S5_P, S5_H), (2 * S5_H) ** -0.5)
    s5_c_re = nrm((L, 2, S5_G, S5_H, S5_P), 0.5)
    s5_c_im = nrm((L, 2, S5_G, S5_H, S5_P), 0.5)
    s5_d = nrm((L, W_S5), 1.0)
    s5_w_glu = nrm((L, W_S5, W_S5), W_S5 ** -0.5)
    s5_b_glu = nrm((L, W_S5), 0.02)
    da_lam = nrm((L, 2, 2, DA_DH), 0.1)
    da_norm_g = 1.0 + nrm((L, DA_DV), 0.02)
    final_g = 1.0 + nrm((D,), 0.02)
    return {'x': x, 'c': c, 'ctx': ctx, 'c_ctx': c_ctx, 'norm_g': norm_g,
            'w_mod': w_mod, 'b_mod': b_mod, 'w_in': w_in, 'w_out': w_out,
            'lru_conv_w': lru_conv_w, 'lru_conv_b': lru_conv_b, 'lru_wa': lru_wa, 'lru_ba': lru_ba,
            'lru_wx': lru_wx, 'lru_bx': lru_bx, 'lru_lam': lru_lam,
            's5_lam_re': s5_lam_re, 's5_lam_im': s5_lam_im, 's5_log_dt': s5_log_dt,
            's5_b_re': s5_b_re, 's5_b_im': s5_b_im, 's5_c_re': s5_c_re, 's5_c_im': s5_c_im,
            's5_d': s5_d, 's5_w_glu': s5_w_glu, 's5_b_glu': s5_b_glu,
            'da_lam': da_lam, 'da_norm_g': da_norm_g, 'final_g': final_g}


def reference(x, c, ctx, c_ctx, norm_g, w_mod, b_mod, w_in, w_out,
              lru_conv_w, lru_conv_b, lru_wa, lru_ba, lru_wx, lru_bx, lru_lam,
              s5_lam_re, s5_lam_im, s5_log_dt, s5_b_re, s5_b_im, s5_c_re, s5_c_im,
              s5_d, s5_w_glu, s5_b_glu, da_lam, da_norm_g, final_g):
    bsz, t, _ = x.shape
    tc = ctx.shape[1]
    rows = t // GRID_W
    row_pos = jnp.broadcast_to(jnp.arange(rows, dtype=F32)[:, None], (rows, GRID_W)).reshape(t)
    col_pos = jnp.broadcast_to(jnp.arange(GRID_W, dtype=F32)[None, :], (rows, GRID_W)).reshape(t)
    inv_freq = jnp.exp(-math.log(ROPE_BASE) * jnp.arange(ROPE_F, dtype=F32) / ROPE_F)
    ang_row = row_pos[:, None] * inv_freq[None, :]
    ang_col = col_pos[:, None] * inv_freq[None, :]

    silu_c = jax.nn.silu(c)
    silu_cc = jax.nn.silu(c_ctx)
    h = x
    hc = ctx
    for l in range(DEPTH):
        last = l == DEPTH - 1
        shift, scale, gate = jnp.split(silu_c @ w_mod[l] + b_mod[l], 3, axis=-1)
        shift_c, scale_c, gate_c = jnp.split(silu_cc @ w_mod[l] + b_mod[l], 3, axis=-1)
        n_lat = rms_norm(h, norm_g[l]) * (1.0 + scale[:, None]) + shift[:, None]
        n_ctx = rms_norm(hc, norm_g[l]) * (1.0 + scale_c) + shift_c
        p_lat = n_lat @ w_in[l]
        p_ctx = n_ctx @ (w_in[l][:, :SIDE_W] if last else w_in[l])
        ua_l, us_l, k_l, v_l = jnp.split(p_lat[..., :SIDE_W], SPLIT4, axis=-1)
        ga_l, gs_l, q_l, gd_l = jnp.split(p_lat[..., SIDE_W:], SPLIT4, axis=-1)
        ua_c, us_c, k_c, v_c = jnp.split(p_ctx[..., :SIDE_W], SPLIT4, axis=-1)

        lru_p = (lru_conv_w[l], lru_conv_b[l], lru_wa[l], lru_ba[l], lru_wx[l], lru_bx[l], lru_lam[l])
        h_lru_c = rglru_scans(ua_c, *lru_p, None)
        h_lru_l = rglru_scans(ua_l, *lru_p, (h_lru_c[0][:, -1], h_lru_c[1][:, 0]))

        disc = [s5_discretise(s5_lam_re[l, d], s5_lam_im[l, d], s5_log_dt[l, d],
                              s5_b_re[l, d], s5_b_im[l, d]) for d in range(2)]
        st_c = [s5_scan(us_c, disc[d], None, d == 1) for d in range(2)]
        st_l = [s5_scan(us_l, disc[d], final_state(st_c[d], d == 1), d == 1) for d in range(2)]

        lam_init = 0.8 - 0.6 * math.exp(-0.3 * l)
        lam = (jnp.exp(jnp.sum(da_lam[l, 0, 0].astype(F32) * da_lam[l, 0, 1].astype(F32)))
               - jnp.exp(jnp.sum(da_lam[l, 1, 0].astype(F32) * da_lam[l, 1, 1].astype(F32))) + lam_init)
        q_l4 = rope_axial(q_l.reshape(bsz, t, DA_HEADS, 2, DA_DH), ang_row, ang_col)
        k_l4 = rope_axial(k_l.reshape(bsz, t, DA_HEADS, 2, DA_DH), ang_row, ang_col)
        k_c4 = k_c.reshape(bsz, tc, DA_HEADS, 2, DA_DH)
        v_c4 = v_c.reshape(bsz, tc, DA_HEADS, DA_DV)
        k_all = jnp.concatenate([k_c4, k_l4], axis=1)
        v_all = jnp.concatenate([v_c4, v_l.reshape(bsz, t, DA_HEADS, DA_DV)], axis=1)
        o_l = latent_diff_attention(q_l4, k_all, v_all, lam)

        s5p = (s5_c_re[l], s5_c_im[l], s5_d[l], s5_w_glu[l], s5_b_glu[l])
        y_l = combine_branches(h_lru_l, st_l, us_l, o_l, ga_l, gs_l, gd_l, *s5p,
                               da_norm_g[l], lam_init, w_out[l])
        h_next = h + gate[:, None] * y_l
        if not last:
            ga_c, gs_c, q_c, gd_c = jnp.split(p_ctx[..., SIDE_W:], SPLIT4, axis=-1)
            o_c = diff_softmax_attend(q_c.reshape(bsz, tc, DA_HEADS, 2, DA_DH), k_c4, v_c4, lam)
            y_c = combine_branches(h_lru_c, st_c, us_c, o_c, ga_c, gs_c, gd_c, *s5p,
                                   da_norm_g[l], lam_init, w_out[l])
            hc = hc + gate_c * y_c
        h = h_next
    return rms_norm(h, final_g)
```

```python
import functools
import math

import jax
import jax.numpy as jnp
from jax import lax
from jax.experimental import pallas as pl
from jax.experimental.pallas import tpu as pltpu

F32 = jnp.float32
BF16 = jnp.bfloat16

GRID_W = 64
EPS = 1e-6
W_LRU = 256
LRU_BLOCKS = 4
CONV_W = 4
LRU_C = 8.0
W_S5 = 256
S5_H = 16
S5_G = 16
S5_P = 64
S5_N = S5_G * S5_P
DA_HEADS = 4
DA_DH = 64
DA_DV = 2 * DA_DH
W_DA = DA_HEADS * DA_DV
ROPE_BASE = 10000.0
ROPE_F = DA_DH // 4
LANES = 128
BATCH = 8

PF_W = 2 * W_LRU + 2 * W_S5 + W_DA
PA_W = 3 * W_DA

TOKEN_TILE = 64
SCAN_TILE = 64
Q_TILE = 256
MOD_ROWS = 16
VMEM_LIMIT = 48 * 1024 * 1024


def _cparams(sem):
    return pltpu.CompilerParams(dimension_semantics=sem, vmem_limit_bytes=VMEM_LIMIT)


def _silu(x):
    return x * jax.nn.sigmoid(x)


def _softplus(x):
    return jnp.maximum(x, 0.0) + jnp.log1p(jnp.exp(-jnp.abs(x)))


def _rope_table_kernel(cos_ref, sin_ref, *, rows, n_ctx):
    i = pl.program_id(0)
    row = lax.broadcasted_iota(jnp.int32, (rows, LANES), 0) + i * rows
    lane = lax.broadcasted_iota(jnp.int32, (rows, LANES), 1)
    t = (row >> int(math.log2(BATCH))) - n_ctx
    is_lat = t >= 0
    use_col = (lane & (DA_DH // 2)) != 0
    f = (lane & (ROPE_F - 1)).astype(F32)
    inv_freq = jnp.exp(-math.log(ROPE_BASE) * f / ROPE_F)
    pos = jnp.where(use_col, t & (GRID_W - 1), t >> int(math.log2(GRID_W)))
    ang = pos.astype(F32) * inv_freq
    cs = jnp.cos(ang)
    sn = jnp.sin(ang)
    first = (lane & ROPE_F) == 0
    cos_ref[...] = jnp.where(is_lat, cs, 1.0)
    sin_ref[...] = jnp.where(is_lat, jnp.where(first, -sn, sn), 0.0)


def _rope_tables(n_rows, n_ctx):
    rows = TOKEN_TILE * BATCH
    return pl.pallas_call(
        functools.partial(_rope_table_kernel, rows=rows, n_ctx=n_ctx),
        grid=(n_rows // rows,),
        out_specs=[pl.BlockSpec((rows, LANES), lambda i: (i, 0))] * 2,
        out_shape=[jax.ShapeDtypeStruct((n_rows, LANES), F32)] * 2,
        compiler_params=_cparams(("arbitrary",)),
    )()


def _mod_kernel(c_ref, w_ref, b_ref, o_ref):
    c = c_ref[...]
    o_ref[0] = jnp.dot(_silu(c).astype(BF16), w_ref[0], preferred_element_type=F32) + b_ref[0]


def _modulation(cc, w_mod, b_mod):
    n_layers, d, d3 = w_mod.shape
    return pl.pallas_call(
        _mod_kernel,
        grid=(n_layers, d3 // d),
        in_specs=[pl.BlockSpec((MOD_ROWS, d), lambda l, j: (0, 0)),
                  pl.BlockSpec((1, d, d), lambda l, j: (l, 0, j)),
                  pl.BlockSpec((1, 1, d), lambda l, j: (l, 0, j))],
        out_specs=pl.BlockSpec((1, MOD_ROWS, d), lambda l, j: (l, 0, j)),
        out_shape=jax.ShapeDtypeStruct((n_layers, MOD_ROWS, d3), F32),
        compiler_params=_cparams(("arbitrary", "arbitrary")),
    )(cc, w_mod, b_mod.reshape(n_layers, 1, d3))


def _rope(x, cos, sin):
    width = x.shape[1]
    reps = width // LANES
    cos = jnp.concatenate([cos] * reps, axis=1)
    sin = jnp.concatenate([sin] * reps, axis=1)
    lane = lax.broadcasted_iota(jnp.int32, x.shape, 1)
    first = (lane & ROPE_F) == 0
    partner = jnp.where(first, pltpu.roll(x, width - ROPE_F, 1), pltpu.roll(x, ROPE_F, 1))
    return x * cos + partner * sin


def _in_proj_kernel(x_ref, mod_ref, g_ref, wf_ref, wa_ref, cos_ref, sin_ref, pf_ref, pa_ref):
    x = x_ref[...]
    rows, d = x.shape
    y = x * lax.rsqrt(jnp.mean(x * x, axis=-1, keepdims=True) + EPS) * g_ref[...]
    shift = mod_ref[0, 0]
    scale = mod_ref[0, 1]
    y3 = y.reshape(rows // BATCH, BATCH, d)
    n = (y3 * (1.0 + scale)[None] + shift[None]).reshape(rows, d).astype(BF16)
    pf_ref[...] = jnp.dot(n, wf_ref[...], preferred_element_type=F32)
    att = jnp.dot(n, wa_ref[...], preferred_element_type=F32)
    cos = cos_ref[...]
    sin = sin_ref[...]
    k = _rope(att[:, :W_DA], cos, sin)
    v = att[:, W_DA:2 * W_DA]
    q = _rope(att[:, 2 * W_DA:], cos, sin) * (DA_DH ** -0.5)
    pa_ref[:, :W_DA] = k.astype(BF16)
    pa_ref[:, W_DA:2 * W_DA] = v.astype(BF16)
    pa_ref[:, 2 * W_DA:] = q.astype(BF16)


def _in_proj(hs, mods, g, wf, wa, cos, sin, *, n_ctx):
    n_rows, d = hs.shape
    tm = TOKEN_TILE * BATCH
    ncb = n_ctx // TOKEN_TILE
    kind = lambda i: (i >= ncb).astype(jnp.int32)
    return pl.pallas_call(
        _in_proj_kernel,
        grid=(n_rows // tm,),
        in_specs=[pl.BlockSpec((tm, d), lambda i: (i, 0)),
                  pl.BlockSpec((1, 3, BATCH, d), lambda i: (kind(i), 0, 0, 0)),
                  pl.BlockSpec((1, d), lambda i: (0, 0)),
                  pl.BlockSpec((d, PF_W), lambda i: (0, 0)),
                  pl.BlockSpec((d, PA_W), lambda i: (0, 0)),
                  pl.BlockSpec((tm, LANES), lambda i: (i, 0)),
                  pl.BlockSpec((tm, LANES), lambda i: (i, 0))],
        out_specs=[pl.BlockSpec((tm, PF_W), lambda i: (i, 0)),
                   pl.BlockSpec((tm, PA_W), lambda i: (i, 0))],
        out_shape=[jax.ShapeDtypeStruct((n_rows, PF_W), F32),
                   jax.ShapeDtypeStruct((n_rows, PA_W), BF16)],
        compiler_params=_cparams(("arbitrary",)),
    )(hs, mods, g.reshape(1, d), wf, wa, cos, sin)


def _bwd_chunk(i, ncc, n):
    return jnp.where(i < ncc, ncc - 1 - i, n - 1 - i + ncc)


def _lru_kernel(ufc, ufp, ufn, ubc, ubp, ubn, cw_ref, cb_ref, wg_ref, bg_ref, lam_ref,
                hf_ref, hb_ref, a_scr, b_scr, carry_scr, *, ts, ncc, n):
    i = pl.program_id(0)

    @pl.when(i == 0)
    def _():
        carry_scr[...] = jnp.zeros_like(carry_scr)

    def prep(d, c, u_c, u_p, u_n):
        seg_start = jnp.logical_or(c == 0, c == ncc)
        seg_end = jnp.logical_or(c == ncc - 1, c == n - 1)
        prev = u_p[...] * jnp.where(seg_start, 0.0, 1.0)
        nxt = u_n[...] * jnp.where(seg_end, 0.0, 1.0)
        u = jnp.concatenate([prev, u_c[...], nxt], axis=0)
        cw = cw_ref[...]
        xc = cb_ref[...] + cw[0:1] * u[0:ts]
        for j in range(1, CONV_W):
            xc = xc + cw[j:j + 1] * u[j:j + ts]
        xc2 = xc.reshape(ts * xc.shape[1], W_LRU)
        gts = jnp.dot(xc2.astype(BF16), wg_ref[d], preferred_element_type=F32) + bg_ref[d]
        gate_r = jax.nn.sigmoid(gts[:, :W_LRU])
        gate_i = jax.nn.sigmoid(gts[:, W_LRU:])
        log_a = -LRU_C * gate_r * _softplus(-lam_ref[d])
        a = jnp.exp(log_a)
        th = jnp.tanh(log_a)
        one_minus_a2 = -2.0 * th / (1.0 - th)
        bb = jnp.sqrt(one_minus_a2) * (gate_i * xc2)
        a_scr[d] = a.reshape(xc.shape)
        b_scr[d] = bb.reshape(xc.shape)

    prep(0, i, ufc, ufp, ufn)
    prep(1, _bwd_chunk(i, ncc, n), ubc, ubp, ubn)

    def body(k, carry):
        hf, hb = carry
        kb = ts - 1 - k
        hf = a_scr[0, k] * hf + b_scr[0, k]
        hb = a_scr[1, kb] * hb + b_scr[1, kb]
        hf_ref[k] = hf
        hb_ref[kb] = hb
        return hf, hb

    hf, hb = lax.fori_loop(0, ts, body, (carry_scr[0], carry_scr[1]), unroll=8)
    carry_scr[0] = hf
    carry_scr[1] = hb


def _lru(pf3, conv_w, conv_b, wg, bg, lam, *, n_ctx):
    s_len, bsz, _ = pf3.shape
    ts = SCAN_TILE
    n = s_len // ts
    ncc = n_ctx // ts

    def cur(cfn):
        return pl.BlockSpec((ts, bsz, W_LRU), lambda i: (cfn(i), 0, 0))

    def prev(cfn):
        return pl.BlockSpec((2, bsz, W_LRU), lambda i: (jnp.maximum(cfn(i) * (ts // 2) - 1, 0), 0, 0))

    def nxt(cfn):
        return pl.BlockSpec((1, bsz, W_LRU), lambda i: (jnp.minimum((cfn(i) + 1) * ts, s_len - 1), 0, 0))

    fwd = lambda i: i
    bwd = lambda i: _bwd_chunk(i, ncc, n)
    whole = lambda shape: pl.BlockSpec(shape, lambda i: (0,) * len(shape))
    return pl.pallas_call(
        functools.partial(_lru_kernel, ts=ts, ncc=ncc, n=n),
        grid=(n,),
        in_specs=[cur(fwd), prev(fwd), nxt(fwd), cur(bwd), prev(bwd), nxt(bwd),
                  whole(conv_w.shape), whole(conv_b.shape), whole(wg.shape), whole(bg.shape),
                  whole(lam.shape)],
        out_specs=[pl.BlockSpec((ts, bsz, W_LRU), lambda i: (i, 0, 0)),
                   pl.BlockSpec((ts, bsz, W_LRU), lambda i: (bwd(i), 0, 0))],
        out_shape=[jax.ShapeDtypeStruct((s_len, bsz, W_LRU), F32)] * 2,
        scratch_shapes=[pltpu.VMEM((2, ts, bsz, W_LRU), F32),
                        pltpu.VMEM((2, ts, bsz, W_LRU), F32),
                        pltpu.VMEM((2, bsz, W_LRU), F32)],
        compiler_params=_cparams(("arbitrary",)),
    )(pf3, pf3, pf3, pf3, pf3, pf3, conv_w, conv_b, wg, bg, lam)


S5_LANE_CHUNK = 512


def _s5_kernel(uf_ref, ub_ref, lre_ref, lim_ref, ldt_ref, bre_ref, bim_ref, cre_ref, cim_ref,
               yf_ref, yb_ref, bbar_scr, a_scr, buf_scr, carry_scr, *, ts):
    i = pl.program_id(0)
    nst = S5_N
    bsz = uf_ref.shape[1]

    @pl.when(i == 0)
    def _():
        carry_scr[...] = jnp.zeros_like(carry_scr)
        for d in range(2):
            lre = lre_ref[d]
            lim = lim_ref[d]
            dt = jnp.exp(ldt_ref[d])
            mag = jnp.exp(lre * dt)
            ang = lim * dt
            ar = mag * jnp.cos(ang)
            ai = mag * jnp.sin(ang)
            nr = ar - 1.0
            den = lre * lre + lim * lim
            cfr = (nr * lre + ai * lim) / den
            cfi = (ai * lre - nr * lim) / den
            bre = bre_ref[d]
            bim = bim_ref[d]
            bbar_scr[d, :, :nst] = (cfr * bre - cfi * bim).astype(BF16)
            bbar_scr[d, :, nst:] = (cfr * bim + cfi * bre).astype(BF16)
            a_scr[d, 0] = jnp.broadcast_to(ar, (bsz, nst))
            a_scr[d, 1] = jnp.broadcast_to(ai, (bsz, nst))

    for d, u_ref in enumerate((uf_ref, ub_ref)):
        u2 = u_ref[...].reshape(ts * bsz, W_S5).astype(BF16)
        bu = jnp.dot(u2, bbar_scr[d], preferred_element_type=F32)
        buf_scr[d] = bu.reshape(ts, bsz, 2 * nst)

    lc = S5_LANE_CHUNK
    for d in range(2):
        for c0 in range(0, nst, lc):
            re_sl = pl.ds(c0, lc)
            im_sl = pl.ds(nst + c0, lc)
            ar = a_scr[d, 0, :, re_sl]
            ai = a_scr[d, 1, :, re_sl]

            def body(k, carry, d=d, re_sl=re_sl, im_sl=im_sl, ar=ar, ai=ai):
                t = k if d == 0 else ts - 1 - k
                sre, sim = carry
                nre = ar * sre - ai * sim + buf_scr[d, t, :, re_sl]
                nim = ar * sim + ai * sre + buf_scr[d, t, :, im_sl]
                buf_scr[d, t, :, re_sl] = nre
                buf_scr[d, t, :, im_sl] = nim
                return nre, nim

            sre, sim = lax.fori_loop(0, ts, body, (carry_scr[d, :, re_sl], carry_scr[d, :, im_sl]),
                                     unroll=4)
            carry_scr[d, :, re_sl] = sre
            carry_scr[d, :, im_sl] = sim

    for d, y_ref in enumerate((yf_ref, yb_ref)):
        st = buf_scr[d].reshape(ts * bsz, 2 * nst)
        y = (jnp.dot(st[:, :nst].astype(BF16), cre_ref[d], preferred_element_type=F32)
             - jnp.dot(st[:, nst:].astype(BF16), cim_ref[d], preferred_element_type=F32))
        y_ref[...] = y.reshape(ts, bsz, W_S5)


def _s5(pf3, lre, lim, ldt, bre, bim, cre, cim, *, n_ctx):
    s_len, bsz, _ = pf3.shape
    ts = SCAN_TILE
    n = s_len // ts
    ncc = n_ctx // ts
    bwd = lambda i: _bwd_chunk(i, ncc, n)
    col = W_LRU // W_S5
    whole = lambda a: pl.BlockSpec(a.shape, lambda i: (0,) * a.ndim)
    return pl.pallas_call(
        functools.partial(_s5_kernel, ts=ts),
        grid=(n,),
        in_specs=[pl.BlockSpec((ts, bsz, W_S5), lambda i: (i, 0, col)),
                  pl.BlockSpec((ts, bsz, W_S5), lambda i: (bwd(i), 0, col)),
                  whole(lre), whole(lim), whole(ldt), whole(bre), whole(bim), whole(cre), whole(cim)],
        out_specs=[pl.BlockSpec((ts, bsz, W_S5), lambda i: (i, 0, 0)),
                   pl.BlockSpec((ts, bsz, W_S5), lambda i: (bwd(i), 0, 0))],
        out_shape=[jax.ShapeDtypeStruct((s_len, bsz, W_S5), F32)] * 2,
        scratch_shapes=[pltpu.VMEM((2, W_S5, 2 * S5_N), BF16),
                        pltpu.VMEM((2, 2, bsz, S5_N), F32),
                        pltpu.VMEM((2, ts, bsz, 2 * S5_N), F32),
                        pltpu.VMEM((2, bsz, 2 * S5_N), F32)],
        compiler_params=_cparams(("arbitrary",)),
    )(pf3, pf3, lre, lim, ldt, bre, bim, cre, cim)


def _attn_kernel(q_ref, k_ref, v_ref, dl_ref, g_ref, o_ref, *, n_ctx_blocks, n_ctx, lam_init):
    qi = pl.program_id(2)
    dl = dl_ref[...]
    lam = (jnp.exp(jnp.sum(dl[0:1] * dl[1:2], axis=-1, keepdims=True))
           - jnp.exp(jnp.sum(dl[2:3] * dl[3:4], axis=-1, keepdims=True)) + lam_init)
    q = q_ref[0]
    lane = lax.broadcasted_iota(jnp.int32, q.shape, 1)
    zero = jnp.zeros_like(q)
    q1 = jnp.where(lane < DA_DH, q, zero)
    q2 = jnp.where(lane >= DA_DH, q, zero)

    def attend(nk):
        k = k_ref[0, 0:nk, :]
        v = v_ref[0, 0:nk, :]

        def soft(qm):
            s = lax.dot_general(qm, k, (((1,), (1,)), ((), ())), preferred_element_type=F32)
            p = jnp.exp(s - jnp.max(s, axis=-1, keepdims=True))
            return p, 1.0 / jnp.sum(p, axis=-1, keepdims=True)

        p1, r1 = soft(q1)
        p2, r2 = soft(q2)
        w = p1 * r1 - p2 * (lam * r2)
        o = jnp.dot(w.astype(BF16), v, preferred_element_type=F32)
        y = o * lax.rsqrt(jnp.mean(o * o, axis=-1, keepdims=True) + EPS) * g_ref[...]
        o_ref[0] = y * (1.0 - lam_init)

    @pl.when(qi < n_ctx_blocks)
    def _():
        attend(n_ctx)

    @pl.when(qi >= n_ctx_blocks)
    def _():
        attend(k_ref.shape[1])


def _attention(pab, da_lam, da_g, *, n_ctx, lam_init):
    bsz, s_len, _ = pab.shape
    tq = Q_TILE
    assert n_ctx % tq == 0
    return pl.pallas_call(
        functools.partial(_attn_kernel, n_ctx_blocks=n_ctx // tq, n_ctx=n_ctx, lam_init=lam_init),
        grid=(bsz, DA_HEADS, s_len // tq),
        in_specs=[pl.BlockSpec((1, tq, LANES), lambda b, h, i: (b, i, 2 * DA_HEADS + h)),
                  pl.BlockSpec((1, s_len, LANES), lambda b, h, i: (b, 0, h)),
                  pl.BlockSpec((1, s_len, LANES), lambda b, h, i: (b, 0, DA_HEADS + h)),
                  pl.BlockSpec((4, DA_DH), lambda b, h, i: (0, 0)),
                  pl.BlockSpec((1, DA_DV), lambda b, h, i: (0, 0))],
        out_specs=pl.BlockSpec((1, tq, LANES), lambda b, h, i: (b, i, h)),
        out_shape=jax.ShapeDtypeStruct((bsz, s_len, W_DA), F32),
        compiler_params=_cparams(("arbitrary", "arbitrary", "arbitrary")),
    )(pab, pab, pab, da_lam.reshape(4, DA_DH), da_g.reshape(1, DA_DV))


def _gelu_tanh(x):
    return 0.5 * x * (1.0 + jnp.tanh(math.sqrt(2.0 / math.pi) * (x + 0.044715 * (x * x * x))))


def _out_proj_kernel(hf_ref, hb_ref, yf_ref, yb_ref, us_ref, ga_ref, gs_ref, gd_ref, yd_ref, h_ref,
                     mod_ref, d_ref, wglu_ref, bglu_ref, wo_ref, fg_ref, o_ref, *, final):
    y_a = (hf_ref[...] + hb_ref[...]) * _silu(ga_ref[...])
    z = _gelu_tanh(d_ref[...] * us_ref[...] + yf_ref[...] + yb_ref[...])
    glu = jax.nn.sigmoid(jnp.dot(z.astype(BF16), wglu_ref[...], preferred_element_type=F32)
                         + bglu_ref[...])
    y_s = z * glu * _silu(gs_ref[...])
    y_d = yd_ref[...] * _silu(gd_ref[...])
    out = (jnp.dot(y_a.astype(BF16), wo_ref[0:W_LRU, :], preferred_element_type=F32)
           + jnp.dot(y_s.astype(BF16), wo_ref[W_LRU:W_LRU + W_S5, :], preferred_element_type=F32)
           + jnp.dot(y_d.astype(BF16), wo_ref[W_LRU + W_S5:, :], preferred_element_type=F32))
    rows, d = out.shape
    gate = mod_ref[0, 2]
    hn = h_ref[...] + (out.reshape(rows // BATCH, BATCH, d) * gate[None]).reshape(rows, d)
    if final:
        hn = hn * lax.rsqrt(jnp.mean(hn * hn, axis=-1, keepdims=True) + EPS) * fg_ref[...]
    o_ref[...] = hn


def _out_proj(hf, hb, yf, yb, pf, yd, hs, mods, s5_d, w_glu, b_glu, w_out, final_g, *, n_ctx, final):
    n_rows, d = hs.shape
    tm = TOKEN_TILE * BATCH
    ncb = n_ctx // TOKEN_TILE
    off = ncb if final else 0
    nblk = n_rows // tm - off
    blk = lambda w, c=0: pl.BlockSpec((tm, w), lambda i: (i + off, c))
    whole = lambda a: pl.BlockSpec(a.shape, lambda i: (0,) * a.ndim)
    kind = lambda i: (i + off >= ncb).astype(jnp.int32)
    s5_d = s5_d.reshape(1, W_S5)
    b_glu = b_glu.reshape(1, W_S5)
    final_g = final_g.reshape(1, d)
    return pl.pallas_call(
        functools.partial(_out_proj_kernel, final=final),
        grid=(nblk,),
        in_specs=[blk(W_LRU), blk(W_LRU), blk(W_S5), blk(W_S5),
                  blk(W_S5, 1), blk(W_LRU, 2), blk(W_S5, 3), blk(W_DA, 2),
                  blk(W_DA), blk(d),
                  pl.BlockSpec((1, 3, BATCH, d), lambda i: (kind(i), 0, 0, 0)),
                  whole(s5_d), whole(w_glu), whole(b_glu), whole(w_out), whole(final_g)],
        out_specs=pl.BlockSpec((tm, d), lambda i: (i, 0)),
        out_shape=jax.ShapeDtypeStruct((nblk * tm, d), F32),
        compiler_params=_cparams(("arbitrary",)),
    )(hf, hb, yf, yb, pf, pf, pf, pf, yd, hs, mods, s5_d, w_glu, b_glu, w_out, final_g)


def _block_diag(w):
    nb, k, n = w.shape
    eye = jnp.eye(nb, dtype=w.dtype)
    return jnp.einsum('nkd,nm->nkmd', w, eye).reshape(nb * k, nb * n)


def kernel(x, c, ctx, c_ctx, norm_g, w_mod, b_mod, w_in, w_out, lru_conv_w, lru_conv_b, lru_wa, lru_ba,
           lru_wx, lru_bx, lru_lam, s5_lam_re, s5_lam_im, s5_log_dt, s5_b_re, s5_b_im, s5_c_re, s5_c_im,
           s5_d, s5_w_glu, s5_b_glu, da_lam, da_norm_g, final_g):
    bsz, t_len, d = x.shape
    n_ctx = ctx.shape[1]
    s_len = n_ctx + t_len
    n_rows = s_len * bsz
    n_layers = w_in.shape[0]
    assert bsz == BATCH and bsz + 1 <= MOD_ROWS
    assert n_ctx % Q_TILE == 0 and t_len % Q_TILE == 0
    assert n_ctx % TOKEN_TILE == 0 and n_ctx % SCAN_TILE == 0

    hs = jnp.concatenate([ctx, x], axis=1).transpose(1, 0, 2).reshape(n_rows, d)

    cc = jnp.concatenate([c, c_ctx[None], jnp.zeros((MOD_ROWS - bsz - 1, d), F32)], axis=0)
    mod = _modulation(cc, w_mod.astype(BF16), b_mod).reshape(n_layers, MOD_ROWS, 3, d)
    mod_lat = mod[:, :bsz].transpose(0, 2, 1, 3)
    mod_ctx = jnp.broadcast_to(mod[:, bsz:bsz + 1], (n_layers, bsz, 3, d)).transpose(0, 2, 1, 3)
    mods = jnp.stack([mod_ctx, mod_lat], axis=1)

    cos, sin = _rope_tables(n_rows, n_ctx)

    side = W_LRU + W_S5 + 2 * W_DA
    c_ua, c_us = (0, W_LRU), (W_LRU, W_LRU + W_S5)
    c_k, c_v = (W_LRU + W_S5, W_LRU + W_S5 + W_DA), (W_LRU + W_S5 + W_DA, side)
    sl = lambda lo_hi, base=0: w_in[:, :, base + lo_hi[0]:base + lo_hi[1]]
    wf = jnp.concatenate([sl(c_ua), sl(c_us), sl(c_ua, side), sl(c_us, side), sl(c_v, side)],
                         axis=-1).astype(BF16)
    wa = jnp.concatenate([sl(c_k), sl(c_v), sl(c_k, side)], axis=-1).astype(BF16)
    w_out_b = w_out.astype(BF16)
    w_glu_b = s5_w_glu.astype(BF16)

    for l in range(n_layers):
        final = l == n_layers - 1
        lam_init = 0.8 - 0.6 * math.exp(-0.3 * l)
        pf, pa = _in_proj(hs, mods[l], norm_g[l], wf[l], wa[l], cos, sin, n_ctx=n_ctx)
        pf3 = pf.reshape(s_len, bsz, PF_W)

        wg = jnp.stack([jnp.concatenate([_block_diag(lru_wa[l, dd]), _block_diag(lru_wx[l, dd])], axis=1)
                        for dd in range(2)]).astype(BF16)
        bg = jnp.concatenate([lru_ba[l], lru_bx[l]], axis=-1).reshape(2, 1, 2 * W_LRU)
        hf, hb = _lru(pf3, lru_conv_w[l], lru_conv_b[l].reshape(1, W_LRU), wg, bg,
                      lru_lam[l].reshape(2, 1, W_LRU), n_ctx=n_ctx)

        flat = lambda a: a.reshape(2, 1, S5_N)
        ldt = jnp.broadcast_to(s5_log_dt[l][:, :, None], (2, S5_G, S5_P))
        bre = jnp.stack([_block_diag(jnp.swapaxes(s5_b_re[l, dd], 1, 2)) for dd in range(2)])
        bim = jnp.stack([_block_diag(jnp.swapaxes(s5_b_im[l, dd], 1, 2)) for dd in range(2)])
        cre = jnp.stack([_block_diag(jnp.swapaxes(s5_c_re[l, dd], 1, 2)) for dd in range(2)]).astype(BF16)
        cim = jnp.stack([_block_diag(jnp.swapaxes(s5_c_im[l, dd], 1, 2)) for dd in range(2)]).astype(BF16)
        yf, yb = _s5(pf3, flat(s5_lam_re[l]), flat(s5_lam_im[l]), flat(ldt), bre, bim, cre, cim,
                     n_ctx=n_ctx)

        pab = pa.reshape(s_len, bsz, PA_W).transpose(1, 0, 2)
        ydb = _attention(pab, da_lam[l], da_norm_g[l], n_ctx=n_ctx, lam_init=lam_init)
        yd = ydb.transpose(1, 0, 2).reshape(n_rows, W_DA)

        hs = _out_proj(hf.reshape(n_rows, W_LRU), hb.reshape(n_rows, W_LRU),
                       yf.reshape(n_rows, W_S5), yb.reshape(n_rows, W_S5),
                       pf, yd, hs, mods[l], s5_d[l], w_glu_b[l], s5_b_glu[l], w_out_b[l], final_g,
                       n_ctx=n_ctx, final=final)
    return hs.reshape(t_len, bsz, d).transpose(1, 0, 2)
```

```python
import functools
import math

import jax
import jax.numpy as jnp
from jax import lax
from jax.experimental import pallas as pl
from jax.experimental.pallas import tpu as pltpu

F32 = jnp.float32
BF16 = jnp.bfloat16

GRID_W = 64
EPS = 1e-6
W_LRU = 256
LRU_BLOCKS = 4
CONV_W = 4
LRU_C = 8.0
W_S5 = 256
S5_H = 16
S5_G = 16
S5_P = 64
S5_N = S5_G * S5_P
DA_HEADS = 4
DA_DH = 64
DA_DV = 2 * DA_DH
W_DA = DA_HEADS * DA_DV
ROPE_BASE = 10000.0
ROPE_F = DA_DH // 4
LANES = 128
BATCH = 8

PF_W = 2 * W_LRU + 2 * W_S5 + W_DA
PA_W = 3 * W_DA

TOKEN_TILE = 64
SCAN_TILE = 64
Q_TILE = 256
MOD_ROWS = 16
VMEM_LIMIT = 48 * 1024 * 1024


def _cparams(sem):
    return pltpu.CompilerParams(dimension_semantics=sem, vmem_limit_bytes=VMEM_LIMIT)


def _silu(x):
    return x * jax.nn.sigmoid(x)


def _softplus(x):
    return jnp.maximum(x, 0.0) + jnp.log1p(jnp.exp(-jnp.abs(x)))


def _rope_table_kernel(cos_ref, sin_ref, *, rows, n_ctx):
    i = pl.program_id(0)
    row = lax.broadcasted_iota(jnp.int32, (rows, LANES), 0) + i * rows
    lane = lax.broadcasted_iota(jnp.int32, (rows, LANES), 1)
    t = (row >> int(math.log2(BATCH))) - n_ctx
    is_lat = t >= 0
    use_col = (lane & (DA_DH // 2)) != 0
    f = (lane & (ROPE_F - 1)).astype(F32)
    inv_freq = jnp.exp(-math.log(ROPE_BASE) * f / ROPE_F)
    pos = jnp.where(use_col, t & (GRID_W - 1), t >> int(math.log2(GRID_W)))
    ang = pos.astype(F32) * inv_freq
    cs = jnp.cos(ang)
    sn = jnp.sin(ang)
    first = (lane & ROPE_F) == 0
    cos_ref[...] = jnp.where(is_lat, cs, 1.0)
    sin_ref[...] = jnp.where(is_lat, jnp.where(first, -sn, sn), 0.0)


def _rope_tables(n_rows, n_ctx):
    rows = TOKEN_TILE * BATCH
    return pl.pallas_call(
        functools.partial(_rope_table_kernel, rows=rows, n_ctx=n_ctx),
        grid=(n_rows // rows,),
        out_specs=[pl.BlockSpec((rows, LANES), lambda i: (i, 0))] * 2,
        out_shape=[jax.ShapeDtypeStruct((n_rows, LANES), F32)] * 2,
        compiler_params=_cparams(("arbitrary",)),
    )()


def _mod_kernel(c_ref, w_ref, b_ref, o_ref):
    c = c_ref[...]
    o_ref[0] = jnp.dot(_silu(c).astype(BF16), w_ref[0], preferred_element_type=F32) + b_ref[0]


def _modulation(cc, w_mod, b_mod):
    n_layers, d, d3 = w_mod.shape
    return pl.pallas_call(
        _mod_kernel,
        grid=(n_layers, d3 // d),
        in_specs=[pl.BlockSpec((MOD_ROWS, d), lambda l, j: (0, 0)),
                  pl.BlockSpec((1, d, d), lambda l, j: (l, 0, j)),
                  pl.BlockSpec((1, 1, d), lambda l, j: (l, 0, j))],
        out_specs=pl.BlockSpec((1, MOD_ROWS, d), lambda l, j: (l, 0, j)),
        out_shape=jax.ShapeDtypeStruct((n_layers, MOD_ROWS, d3), F32),
        compiler_params=_cparams(("arbitrary", "arbitrary")),
    )(cc, w_mod, b_mod.reshape(n_layers, 1, d3))


def _rope(x, cos, sin):
    width = x.shape[1]
    reps = width // LANES
    cos = jnp.concatenate([cos] * reps, axis=1)
    sin = jnp.concatenate([sin] * reps, axis=1)
    lane = lax.broadcasted_iota(jnp.int32, x.shape, 1)
    first = (lane & ROPE_F) == 0
    partner = jnp.where(first, pltpu.roll(x, width - ROPE_F, 1), pltpu.roll(x, ROPE_F, 1))
    return x * cos + partner * sin


def _in_proj_kernel(x_ref, mod_ref, g_ref, wf_ref, wa_ref, cos_ref, sin_ref, pf_ref, pa_ref):
    x = x_ref[...]
    rows, d = x.shape
    y = x * lax.rsqrt(jnp.mean(x * x, axis=-1, keepdims=True) + EPS) * g_ref[...]
    shift = mod_ref[0, 0]
    scale = mod_ref[0, 1]
    y3 = y.reshape(rows // BATCH, BATCH, d)
    n = (y3 * (1.0 + scale)[None] + shift[None]).reshape(rows, d).astype(BF16)
    pf_ref[...] = jnp.dot(n, wf_ref[...], preferred_element_type=F32)
    att = jnp.dot(n, wa_ref[...], preferred_element_type=F32)
    cos = cos_ref[...]
    sin = sin_ref[...]
    k = _rope(att[:, :W_DA], cos, sin)
    v = att[:, W_DA:2 * W_DA]
    q = _rope(att[:, 2 * W_DA:], cos, sin) * (DA_DH ** -0.5 * math.log2(math.e))
    pa_ref[:, :W_DA] = k.astype(BF16)
    pa_ref[:, W_DA:2 * W_DA] = v.astype(BF16)
    pa_ref[:, 2 * W_DA:] = q.astype(BF16)


def _in_proj(hs, mods, g, wf, wa, cos, sin, *, n_ctx):
    n_rows, d = hs.shape
    tm = TOKEN_TILE * BATCH
    ncb = n_ctx // TOKEN_TILE
    kind = lambda i: (i >= ncb).astype(jnp.int32)
    return pl.pallas_call(
        _in_proj_kernel,
        grid=(n_rows // tm,),
        in_specs=[pl.BlockSpec((tm, d), lambda i: (i, 0)),
                  pl.BlockSpec((1, 3, BATCH, d), lambda i: (kind(i), 0, 0, 0)),
                  pl.BlockSpec((1, d), lambda i: (0, 0)),
                  pl.BlockSpec((d, PF_W), lambda i: (0, 0)),
                  pl.BlockSpec((d, PA_W), lambda i: (0, 0)),
                  pl.BlockSpec((tm, LANES), lambda i: (i, 0)),
                  pl.BlockSpec((tm, LANES), lambda i: (i, 0))],
        out_specs=[pl.BlockSpec((tm, PF_W), lambda i: (i, 0)),
                   pl.BlockSpec((tm, PA_W), lambda i: (i, 0))],
        out_shape=[jax.ShapeDtypeStruct((n_rows, PF_W), F32),
                   jax.ShapeDtypeStruct((n_rows, PA_W), BF16)],
        compiler_params=_cparams(("arbitrary",)),
    )(hs, mods, g.reshape(1, d), wf, wa, cos, sin)


def _bwd_chunk(i, ncc, n):
    return jnp.where(i < ncc, ncc - 1 - i, n - 1 - i + ncc)


def _lru_kernel(ufc, ufp, ufn, ubc, ubp, ubn, cw_ref, cb_ref, wg_ref, bg_ref, lam_ref,
                hf_ref, hb_ref, a_scr, b_scr, carry_scr, *, ts, ncc, n):
    i = pl.program_id(0)

    @pl.when(i == 0)
    def _():
        carry_scr[...] = jnp.zeros_like(carry_scr)

    def prep(d, c, u_c, u_p, u_n):
        seg_start = jnp.logical_or(c == 0, c == ncc)
        seg_end = jnp.logical_or(c == ncc - 1, c == n - 1)
        prev = u_p[...] * jnp.where(seg_start, 0.0, 1.0)
        nxt = u_n[...] * jnp.where(seg_end, 0.0, 1.0)
        u = jnp.concatenate([prev, u_c[...], nxt], axis=0)
        cw = cw_ref[...]
        xc = cb_ref[...] + cw[0:1] * u[0:ts]
        for j in range(1, CONV_W):
            xc = xc + cw[j:j + 1] * u[j:j + ts]
        xc2 = xc.reshape(ts * xc.shape[1], W_LRU)
        gts = jnp.dot(xc2.astype(BF16), wg_ref[d], preferred_element_type=F32) + bg_ref[d]
        gate_r = jax.nn.sigmoid(gts[:, :W_LRU])
        gate_i = jax.nn.sigmoid(gts[:, W_LRU:])
        log_a = -LRU_C * gate_r * _softplus(-lam_ref[d])
        a = jnp.exp(log_a)
        th = jnp.tanh(log_a)
        one_minus_a2 = -2.0 * th / (1.0 - th)
        bb = jnp.sqrt(one_minus_a2) * (gate_i * xc2)
        a_scr[d] = a.reshape(xc.shape)
        b_scr[d] = bb.reshape(xc.shape)

    prep(0, i, ufc, ufp, ufn)
    prep(1, _bwd_chunk(i, ncc, n), ubc, ubp, ubn)

    def body(k, carry):
        hf, hb = carry
        kb = ts - 1 - k
        hf = a_scr[0, k] * hf + b_scr[0, k]
        hb = a_scr[1, kb] * hb + b_scr[1, kb]
        hf_ref[k] = hf
        hb_ref[kb] = hb
        return hf, hb

    hf, hb = lax.fori_loop(0, ts, body, (carry_scr[0], carry_scr[1]), unroll=8)
    carry_scr[0] = hf
    carry_scr[1] = hb


def _lru(pf3, conv_w, conv_b, wg, bg, lam, *, n_ctx):
    s_len, bsz, _ = pf3.shape
    ts = SCAN_TILE
    n = s_len // ts
    ncc = n_ctx // ts

    def cur(cfn):
        return pl.BlockSpec((ts, bsz, W_LRU), lambda i: (cfn(i), 0, 0))

    def prev(cfn):
        return pl.BlockSpec((2, bsz, W_LRU), lambda i: (jnp.maximum(cfn(i) * (ts // 2) - 1, 0), 0, 0))

    def nxt(cfn):
        return pl.BlockSpec((1, bsz, W_LRU), lambda i: (jnp.minimum((cfn(i) + 1) * ts, s_len - 1), 0, 0))

    fwd = lambda i: i
    bwd = lambda i: _bwd_chunk(i, ncc, n)
    whole = lambda shape: pl.BlockSpec(shape, lambda i: (0,) * len(shape))
    return pl.pallas_call(
        functools.partial(_lru_kernel, ts=ts, ncc=ncc, n=n),
        grid=(n,),
        in_specs=[cur(fwd), prev(fwd), nxt(fwd), cur(bwd), prev(bwd), nxt(bwd),
                  whole(conv_w.shape), whole(conv_b.shape), whole(wg.shape), whole(bg.shape),
                  whole(lam.shape)],
        out_specs=[pl.BlockSpec((ts, bsz, W_LRU), lambda i: (i, 0, 0)),
                   pl.BlockSpec((ts, bsz, W_LRU), lambda i: (bwd(i), 0, 0))],
        out_shape=[jax.ShapeDtypeStruct((s_len, bsz, W_LRU), F32)] * 2,
        scratch_shapes=[pltpu.VMEM((2, ts, bsz, W_LRU), F32),
                        pltpu.VMEM((2, ts, bsz, W_LRU), F32),
                        pltpu.VMEM((2, bsz, W_LRU), F32)],
        compiler_params=_cparams(("arbitrary",)),
    )(pf3, pf3, pf3, pf3, pf3, pf3, conv_w, conv_b, wg, bg, lam)


S5_LANE_CHUNK = 512


def _s5_kernel(uf_ref, ub_ref, lre_ref, lim_ref, ldt_ref, bre_ref, bim_ref, cre_ref, cim_ref,
               yf_ref, yb_ref, bbar_scr, a_scr, buf_scr, carry_scr, *, ts):
    i = pl.program_id(0)
    nst = S5_N
    bsz = uf_ref.shape[1]

    @pl.when(i == 0)
    def _():
        carry_scr[...] = jnp.zeros_like(carry_scr)
        for d in range(2):
            lre = lre_ref[d]
            lim = lim_ref[d]
            dt = jnp.exp(ldt_ref[d])
            mag = jnp.exp(lre * dt)
            ang = lim * dt
            ar = mag * jnp.cos(ang)
            ai = mag * jnp.sin(ang)
            nr = ar - 1.0
            den = lre * lre + lim * lim
            cfr = (nr * lre + ai * lim) / den
            cfi = (ai * lre - nr * lim) / den
            bre = bre_ref[d]
            bim = bim_ref[d]
            bbar_scr[d, :, :nst] = (cfr * bre - cfi * bim).astype(BF16)
            bbar_scr[d, :, nst:] = (cfr * bim + cfi * bre).astype(BF16)
            a_scr[d, 0] = jnp.broadcast_to(ar, (bsz, nst))
            a_scr[d, 1] = jnp.broadcast_to(ai, (bsz, nst))

    for d, u_ref in enumerate((uf_ref, ub_ref)):
        u2 = u_ref[...].reshape(ts * bsz, W_S5).astype(BF16)
        bu = jnp.dot(u2, bbar_scr[d], preferred_element_type=F32)
        buf_scr[d] = bu.reshape(ts, bsz, 2 * nst)

    lc = S5_LANE_CHUNK
    for d in range(2):
        for c0 in range(0, nst, lc):
            re_sl = pl.ds(c0, lc)
            im_sl = pl.ds(nst + c0, lc)
            ar = a_scr[d, 0, :, re_sl]
            ai = a_scr[d, 1, :, re_sl]

            def body(k, carry, d=d, re_sl=re_sl, im_sl=im_sl, ar=ar, ai=ai):
                t = k if d == 0 else ts - 1 - k
                sre, sim = carry
                nre = ar * sre - ai * sim + buf_scr[d, t, :, re_sl]
                nim = ar * sim + ai * sre + buf_scr[d, t, :, im_sl]
                buf_scr[d, t, :, re_sl] = nre
                buf_scr[d, t, :, im_sl] = nim
                return nre, nim

            sre, sim = lax.fori_loop(0, ts, body, (carry_scr[d, :, re_sl], carry_scr[d, :, im_sl]),
                                     unroll=4)
            carry_scr[d, :, re_sl] = sre
            carry_scr[d, :, im_sl] = sim

    for d, y_ref in enumerate((yf_ref, yb_ref)):
        st = buf_scr[d].reshape(ts * bsz, 2 * nst)
        y = (jnp.dot(st[:, :nst].astype(BF16), cre_ref[d], preferred_element_type=F32)
             - jnp.dot(st[:, nst:].astype(BF16), cim_ref[d], preferred_element_type=F32))
        y_ref[...] = y.reshape(ts, bsz, W_S5)


def _s5(pf3, lre, lim, ldt, bre, bim, cre, cim, *, n_ctx):
    s_len, bsz, _ = pf3.shape
    ts = SCAN_TILE
    n = s_len // ts
    ncc = n_ctx // ts
    bwd = lambda i: _bwd_chunk(i, ncc, n)
    col = W_LRU // W_S5
    whole = lambda a: pl.BlockSpec(a.shape, lambda i: (0,) * a.ndim)
    return pl.pallas_call(
        functools.partial(_s5_kernel, ts=ts),
        grid=(n,),
        in_specs=[pl.BlockSpec((ts, bsz, W_S5), lambda i: (i, 0, col)),
                  pl.BlockSpec((ts, bsz, W_S5), lambda i: (bwd(i), 0, col)),
                  whole(lre), whole(lim), whole(ldt), whole(bre), whole(bim), whole(cre), whole(cim)],
        out_specs=[pl.BlockSpec((ts, bsz, W_S5), lambda i: (i, 0, 0)),
                   pl.BlockSpec((ts, bsz, W_S5), lambda i: (bwd(i), 0, 0))],
        out_shape=[jax.ShapeDtypeStruct((s_len, bsz, W_S5), F32)] * 2,
        scratch_shapes=[pltpu.VMEM((2, W_S5, 2 * S5_N), BF16),
                        pltpu.VMEM((2, 2, bsz, S5_N), F32),
                        pltpu.VMEM((2, ts, bsz, 2 * S5_N), F32),
                        pltpu.VMEM((2, bsz, 2 * S5_N), F32)],
        compiler_params=_cparams(("arbitrary",)),
    )(pf3, pf3, lre, lim, ldt, bre, bim, cre, cim)


V_ROWS = DA_DV + 16


def _attn_kernel(q_ref, k_ref, vt_ref, dl_ref, g_ref, o_ref, s_scr, *, tq, n_ctx, lam_init):
    s_len = k_ref.shape[1]
    nq = s_len // tq
    ncq = n_ctx // tq
    dl = dl_ref[...]
    lam = (jnp.exp(jnp.sum(dl[0:1] * dl[1:2], axis=-1, keepdims=True))
           - jnp.exp(jnp.sum(dl[2:3] * dl[3:4], axis=-1, keepdims=True)) + lam_init)
    gscale = g_ref[...] * (1.0 - lam_init)
    lane = lax.broadcasted_iota(jnp.int32, (tq, LANES), 1)

    def rows(qt):
        return pl.ds(pl.multiple_of(qt * tq, tq), tq)

    def scores(qt, mp, nk):
        q = q_ref[0, rows(qt), :]
        qm = jnp.where((lane < DA_DH) if mp == 0 else (lane >= DA_DH), q, jnp.zeros_like(q))
        s_t = lax.dot_general(k_ref[0, 0:nk, :], qm, (((1,), (1,)), ((), ())),
                              preferred_element_type=F32)
        s_scr[mp, 0:nk, :] = s_t
        return jnp.max(s_t, axis=0, keepdims=True)

    def values(mp, m, nk):
        p_t = jnp.exp2(s_scr[mp, 0:nk, :] - m).astype(BF16)
        return jnp.dot(vt_ref[0, 0, :, 0:nk], p_t, preferred_element_type=F32)

    def finish(qt, o1, o2):
        o_t = (o1[:DA_DV] * (1.0 / o1[DA_DV:DA_DV + 1])
               - o2[:DA_DV] * (lam / o2[DA_DV:DA_DV + 1]))
        y_t = o_t * lax.rsqrt(jnp.mean(o_t * o_t, axis=0, keepdims=True) + EPS)
        o_ref[0, rows(qt), :] = y_t.T * gscale

    for qt in range(ncq):
        m0 = scores(qt, 0, n_ctx)
        m1 = scores(qt, 1, n_ctx)
        finish(qt, values(0, m0, n_ctx), values(1, m1, n_ctx))

    def body(qt, m0):
        m1 = scores(qt, 1, s_len)
        o1 = values(0, m0, s_len)
        m0_next = scores(qt + 1, 0, s_len)
        o2 = values(1, m1, s_len)
        finish(qt, o1, o2)
        return m0_next

    m0 = lax.fori_loop(ncq, nq - 1, body, scores(ncq, 0, s_len))
    m1 = scores(nq - 1, 1, s_len)
    finish(nq - 1, values(0, m0, s_len), values(1, m1, s_len))


def _attention(qk, vt, da_lam, da_g, *, n_ctx, lam_init):
    bsz, s_len, _ = qk.shape
    tq = Q_TILE
    assert n_ctx % tq == 0 and s_len // tq >= n_ctx // tq + 2
    return pl.pallas_call(
        functools.partial(_attn_kernel, tq=tq, n_ctx=n_ctx, lam_init=lam_init),
        grid=(bsz, DA_HEADS),
        in_specs=[pl.BlockSpec((1, s_len, LANES), lambda b, h: (b, 0, DA_HEADS + h)),
                  pl.BlockSpec((1, s_len, LANES), lambda b, h: (b, 0, h)),
                  pl.BlockSpec((1, 1, V_ROWS, s_len), lambda b, h: (b, h, 0, 0)),
                  pl.BlockSpec((4, DA_DH), lambda b, h: (0, 0)),
                  pl.BlockSpec((1, DA_DV), lambda b, h: (0, 0))],
        out_specs=pl.BlockSpec((1, s_len, LANES), lambda b, h: (b, 0, h)),
        out_shape=jax.ShapeDtypeStruct((bsz, s_len, W_DA), F32),
        scratch_shapes=[pltpu.VMEM((2, s_len, tq), F32)],
        compiler_params=_cparams(("arbitrary", "arbitrary")),
    )(qk, qk, vt, da_lam.reshape(4, DA_DH), da_g.reshape(1, DA_DV))


def _gelu_tanh(x):
    return 0.5 * x * (1.0 + jnp.tanh(math.sqrt(2.0 / math.pi) * (x + 0.044715 * (x * x * x))))


def _out_proj_kernel(hf_ref, hb_ref, yf_ref, yb_ref, us_ref, ga_ref, gs_ref, gd_ref, yd_ref, h_ref,
                     mod_ref, d_ref, wglu_ref, bglu_ref, wo_ref, fg_ref, o_ref, *, final):
    y_a = (hf_ref[...] + hb_ref[...]) * _silu(ga_ref[...])
    z = _gelu_tanh(d_ref[...] * us_ref[...] + yf_ref[...] + yb_ref[...])
    glu = jax.nn.sigmoid(jnp.dot(z.astype(BF16), wglu_ref[...], preferred_element_type=F32)
                         + bglu_ref[...])
    y_s = z * glu * _silu(gs_ref[...])
    y_d = yd_ref[...] * _silu(gd_ref[...])
    out = (jnp.dot(y_a.astype(BF16), wo_ref[0:W_LRU, :], preferred_element_type=F32)
           + jnp.dot(y_s.astype(BF16), wo_ref[W_LRU:W_LRU + W_S5, :], preferred_element_type=F32)
           + jnp.dot(y_d.astype(BF16), wo_ref[W_LRU + W_S5:, :], preferred_element_type=F32))
    rows, d = out.shape
    gate = mod_ref[0, 2]
    hn = h_ref[...] + (out.reshape(rows // BATCH, BATCH, d) * gate[None]).reshape(rows, d)
    if final:
        hn = hn * lax.rsqrt(jnp.mean(hn * hn, axis=-1, keepdims=True) + EPS) * fg_ref[...]
    o_ref[...] = hn


def _out_proj(hf, hb, yf, yb, pf, yd, hs, mods, s5_d, w_glu, b_glu, w_out, final_g, *, n_ctx, final):
    n_rows, d = hs.shape
    tm = TOKEN_TILE * BATCH
    ncb = n_ctx // TOKEN_TILE
    off = ncb if final else 0
    nblk = n_rows // tm - off
    blk = lambda w, c=0: pl.BlockSpec((tm, w), lambda i: (i + off, c))
    whole = lambda a: pl.BlockSpec(a.shape, lambda i: (0,) * a.ndim)
    kind = lambda i: (i + off >= ncb).astype(jnp.int32)
    s5_d = s5_d.reshape(1, W_S5)
    b_glu = b_glu.reshape(1, W_S5)
    final_g = final_g.reshape(1, d)
    return pl.pallas_call(
        functools.partial(_out_proj_kernel, final=final),
        grid=(nblk,),
        in_specs=[blk(W_LRU), blk(W_LRU), blk(W_S5), blk(W_S5),
                  blk(W_S5, 1), blk(W_LRU, 2), blk(W_S5, 3), blk(W_DA, 2),
                  blk(W_DA), blk(d),
                  pl.BlockSpec((1, 3, BATCH, d), lambda i: (kind(i), 0, 0, 0)),
                  whole(s5_d), whole(w_glu), whole(b_glu), whole(w_out), whole(final_g)],
        out_specs=pl.BlockSpec((tm, d), lambda i: (i, 0)),
        out_shape=jax.ShapeDtypeStruct((nblk * tm, d), F32),
        compiler_params=_cparams(("arbitrary",)),
    )(hf, hb, yf, yb, pf, pf, pf, pf, yd, hs, mods, s5_d, w_glu, b_glu, w_out, final_g)


def _block_diag(w):
    nb, k, n = w.shape
    eye = jnp.eye(nb, dtype=w.dtype)
    return jnp.einsum('nkd,nm->nkmd', w, eye).reshape(nb * k, nb * n)


def kernel(x, c, ctx, c_ctx, norm_g, w_mod, b_mod, w_in, w_out, lru_conv_w, lru_conv_b, lru_wa, lru_ba,
           lru_wx, lru_bx, lru_lam, s5_lam_re, s5_lam_im, s5_log_dt, s5_b_re, s5_b_im, s5_c_re, s5_c_im,
           s5_d, s5_w_glu, s5_b_glu, da_lam, da_norm_g, final_g):
    bsz, t_len, d = x.shape
    n_ctx = ctx.shape[1]
    s_len = n_ctx + t_len
    n_rows = s_len * bsz
    n_layers = w_in.shape[0]
    assert bsz == BATCH and bsz + 1 <= MOD_ROWS
    assert n_ctx % Q_TILE == 0 and t_len % Q_TILE == 0
    assert n_ctx % TOKEN_TILE == 0 and n_ctx % SCAN_TILE == 0

    hs = jnp.concatenate([ctx, x], axis=1).transpose(1, 0, 2).reshape(n_rows, d)

    cc = jnp.concatenate([c, c_ctx[None], jnp.zeros((MOD_ROWS - bsz - 1, d), F32)], axis=0)
    mod = _modulation(cc, w_mod.astype(BF16), b_mod).reshape(n_layers, MOD_ROWS, 3, d)
    mod_lat = mod[:, :bsz].transpose(0, 2, 1, 3)
    mod_ctx = jnp.broadcast_to(mod[:, bsz:bsz + 1], (n_layers, bsz, 3, d)).transpose(0, 2, 1, 3)
    mods = jnp.stack([mod_ctx, mod_lat], axis=1)

    cos, sin = _rope_tables(n_rows, n_ctx)

    side = W_LRU + W_S5 + 2 * W_DA
    c_ua, c_us = (0, W_LRU), (W_LRU, W_LRU + W_S5)
    c_k, c_v = (W_LRU + W_S5, W_LRU + W_S5 + W_DA), (W_LRU + W_S5 + W_DA, side)
    sl = lambda lo_hi, base=0: w_in[:, :, base + lo_hi[0]:base + lo_hi[1]]
    wf = jnp.concatenate([sl(c_ua), sl(c_us), sl(c_ua, side), sl(c_us, side), sl(c_v, side)],
                         axis=-1).astype(BF16)
    wa = jnp.concatenate([sl(c_k), sl(c_v), sl(c_k, side)], axis=-1).astype(BF16)
    w_out_b = w_out.astype(BF16)
    w_glu_b = s5_w_glu.astype(BF16)

    for l in range(n_layers):
        final = l == n_layers - 1
        lam_init = 0.8 - 0.6 * math.exp(-0.3 * l)
        pf, pa = _in_proj(hs, mods[l], norm_g[l], wf[l], wa[l], cos, sin, n_ctx=n_ctx)
        pf3 = pf.reshape(s_len, bsz, PF_W)

        wg = jnp.stack([jnp.concatenate([_block_diag(lru_wa[l, dd]), _block_diag(lru_wx[l, dd])], axis=1)
                        for dd in range(2)]).astype(BF16)
        bg = jnp.concatenate([lru_ba[l], lru_bx[l]], axis=-1).reshape(2, 1, 2 * W_LRU)
        hf, hb = _lru(pf3, lru_conv_w[l], lru_conv_b[l].reshape(1, W_LRU), wg, bg,
                      lru_lam[l].reshape(2, 1, W_LRU), n_ctx=n_ctx)

        flat = lambda a: a.reshape(2, 1, S5_N)
        ldt = jnp.broadcast_to(s5_log_dt[l][:, :, None], (2, S5_G, S5_P))
        bre = jnp.stack([_block_diag(jnp.swapaxes(s5_b_re[l, dd], 1, 2)) for dd in range(2)])
        bim = jnp.stack([_block_diag(jnp.swapaxes(s5_b_im[l, dd], 1, 2)) for dd in range(2)])
        cre = jnp.stack([_block_diag(jnp.swapaxes(s5_c_re[l, dd], 1, 2)) for dd in range(2)]).astype(BF16)
        cim = jnp.stack([_block_diag(jnp.swapaxes(s5_c_im[l, dd], 1, 2)) for dd in range(2)]).astype(BF16)
        yf, yb = _s5(pf3, flat(s5_lam_re[l]), flat(s5_lam_im[l]), flat(ldt), bre, bim, cre, cim,
                     n_ctx=n_ctx)

        pa3 = pa.reshape(s_len, bsz, PA_W)
        qk = jnp.concatenate([pa3[:, :, :W_DA], pa3[:, :, 2 * W_DA:]], axis=-1).transpose(1, 0, 2)
        vt = pa3[:, :, W_DA:2 * W_DA].transpose(1, 2, 0).reshape(bsz, DA_HEADS, DA_DV, s_len)
        vt = jnp.concatenate([vt, jnp.ones((bsz, DA_HEADS, 1, s_len), BF16),
                              jnp.zeros((bsz, DA_HEADS, V_ROWS - DA_DV - 1, s_len), BF16)], axis=2)
        ydb = _attention(qk, vt, da_lam[l], da_norm_g[l], n_ctx=n_ctx, lam_init=lam_init)
        yd = ydb.transpose(1, 0, 2).reshape(n_rows, W_DA)

        hs = _out_proj(hf.reshape(n_rows, W_LRU), hb.reshape(n_rows, W_LRU),
                       yf.reshape(n_rows, W_S5), yb.reshape(n_rows, W_S5),
                       pf, yd, hs, mods[l], s5_d[l], w_glu_b[l], s5_b_glu[l], w_out_b[l], final_g,
                       n_ctx=n_ctx, final=final)
    return hs.reshape(t_len, bsz, d).transpose(1, 0, 2)
```

```python
import functools
import math

import jax
import jax.numpy as jnp
from jax import lax
from jax.experimental import pallas as pl
from jax.experimental.pallas import tpu as pltpu

F32 = jnp.float32
BF16 = jnp.bfloat16

GRID_W = 64
EPS = 1e-6
W_LRU = 256
LRU_BLOCKS = 4
CONV_W = 4
LRU_C = 8.0
W_S5 = 256
S5_H = 16
S5_G = 16
S5_P = 64
S5_N = S5_G * S5_P
DA_HEADS = 4
DA_DH = 64
DA_DV = 2 * DA_DH
W_DA = DA_HEADS * DA_DV
ROPE_BASE = 10000.0
ROPE_F = DA_DH // 4
LANES = 128
BATCH = 8

PF_W = 2 * W_LRU + 2 * W_S5 + W_DA
PA_W = 3 * W_DA

TOKEN_TILE = 64
SCAN_TILE = 64
Q_TILE = 256
MOD_ROWS = 16
VMEM_LIMIT = 48 * 1024 * 1024


def _cparams(sem):
    return pltpu.CompilerParams(dimension_semantics=sem, vmem_limit_bytes=VMEM_LIMIT)


def _silu(x):
    return x * jax.nn.sigmoid(x)


def _softplus(x):
    return jnp.maximum(x, 0.0) + jnp.log1p(jnp.exp(-jnp.abs(x)))


def _rows_to_batch(val, scr, slab0, out_ref, col0, dtype):
    rows, width = val.shape
    tt = rows // BATCH
    for j in range(width // LANES):
        scr[slab0 + j] = val[:, j * LANES:(j + 1) * LANES]
    for b in range(BATCH):
        for j in range(width // LANES):
            out_ref[b, :, col0 + j * LANES:col0 + (j + 1) * LANES] = (
                scr[slab0 + j, pl.ds(b, tt, stride=BATCH), :].astype(dtype))


def _batch_to_rows(in_ref, scr):
    _, tt, width = in_ref.shape
    for b in range(BATCH):
        for j in range(width // LANES):
            scr[j, pl.ds(b, tt, stride=BATCH), :] = in_ref[b, :, j * LANES:(j + 1) * LANES].astype(F32)
    return jnp.concatenate([scr[j] for j in range(width // LANES)], axis=1)


def _to_rows_kernel(ctx_ref, x_ref, o_ref, scr, *, ncb):
    i = pl.program_id(0)

    @pl.when(i < ncb)
    def _():
        o_ref[...] = _batch_to_rows(ctx_ref, scr)

    @pl.when(i >= ncb)
    def _():
        o_ref[...] = _batch_to_rows(x_ref, scr)


def _to_rows(ctx, x):
    bsz, n_ctx, d = ctx.shape
    t_len = x.shape[1]
    tt = TOKEN_TILE
    ncb = n_ctx // tt
    n = ncb + t_len // tt
    return pl.pallas_call(
        functools.partial(_to_rows_kernel, ncb=ncb),
        grid=(n,),
        in_specs=[pl.BlockSpec((bsz, tt, d), lambda i: (0, jnp.minimum(i, ncb - 1), 0)),
                  pl.BlockSpec((bsz, tt, d), lambda i: (0, jnp.maximum(i - ncb, 0), 0))],
        out_specs=pl.BlockSpec((tt * bsz, d), lambda i: (i, 0)),
        out_shape=jax.ShapeDtypeStruct((n * tt * bsz, d), F32),
        scratch_shapes=[pltpu.VMEM((d // LANES, tt * bsz, LANES), F32)],
        compiler_params=_cparams(("arbitrary",)),
    )(ctx, x)


def _rope_table_kernel(csp_ref, snp_ref, csr_ref, snr_ref):
    def table(shape, pos_shift):
        pos = lax.broadcasted_iota(jnp.int32, shape, 0) >> pos_shift
        lane = lax.broadcasted_iota(jnp.int32, shape, 1)
        f = (lane & (ROPE_F - 1)).astype(F32)
        ang = pos.astype(F32) * jnp.exp(-math.log(ROPE_BASE) * f / ROPE_F)
        sn = jnp.sin(ang)
        return jnp.cos(ang), jnp.where((lane & ROPE_F) == 0, -sn, sn)

    csp_ref[...], snp_ref[...] = table(csp_ref.shape, 0)
    csr_ref[...], snr_ref[...] = table(csr_ref.shape, int(math.log2(BATCH)))


def _rope_tables():
    shapes = [(GRID_W, LANES)] * 2 + [(GRID_W * BATCH, LANES)] * 2
    return pl.pallas_call(
        _rope_table_kernel,
        out_shape=[jax.ShapeDtypeStruct(s, F32) for s in shapes],
        compiler_params=pltpu.CompilerParams(vmem_limit_bytes=VMEM_LIMIT),
    )()


def _mod_kernel(c_ref, w_ref, b_ref, o_ref):
    c = c_ref[...]
    o_ref[0] = jnp.dot(_silu(c).astype(BF16), w_ref[0], preferred_element_type=F32) + b_ref[0]


def _modulation(cc, w_mod, b_mod):
    n_layers, d, d3 = w_mod.shape
    return pl.pallas_call(
        _mod_kernel,
        grid=(n_layers, d3 // d),
        in_specs=[pl.BlockSpec((MOD_ROWS, d), lambda l, j: (0, 0)),
                  pl.BlockSpec((1, d, d), lambda l, j: (l, 0, j)),
                  pl.BlockSpec((1, 1, d), lambda l, j: (l, 0, j))],
        out_specs=pl.BlockSpec((1, MOD_ROWS, d), lambda l, j: (l, 0, j)),
        out_shape=jax.ShapeDtypeStruct((n_layers, MOD_ROWS, d3), F32),
        compiler_params=_cparams(("arbitrary", "arbitrary")),
    )(cc, w_mod, b_mod.reshape(n_layers, 1, d3))


def _rope(x, cos, sin):
    width = x.shape[1]
    reps = width // LANES
    cos = jnp.concatenate([cos] * reps, axis=1)
    sin = jnp.concatenate([sin] * reps, axis=1)
    lane = lax.broadcasted_iota(jnp.int32, x.shape, 1)
    first = (lane & ROPE_F) == 0
    partner = jnp.where(first, pltpu.roll(x, width - ROPE_F, 1), pltpu.roll(x, ROPE_F, 1))
    return x * cos + partner * sin


def _in_proj_kernel(x_ref, mod_ref, g_ref, wf_ref, wa_ref, csp_ref, snp_ref, csr_ref, snr_ref,
                    pf_ref, kq_ref, v_ref, scr, *, ncb):
    i = pl.program_id(0)
    x = x_ref[...]
    rows, d = x.shape
    y = x * lax.rsqrt(jnp.mean(x * x, axis=-1, keepdims=True) + EPS) * g_ref[...]
    shift = mod_ref[0, 0]
    scale = mod_ref[0, 1]
    y3 = y.reshape(rows // BATCH, BATCH, d)
    n = (y3 * (1.0 + scale)[None] + shift[None]).reshape(rows, d).astype(BF16)
    pf_ref[...] = jnp.dot(n, wf_ref[...], preferred_element_type=F32)
    att = jnp.dot(n, wa_ref[...], preferred_element_type=F32)
    grid_row = jnp.maximum(i - ncb, 0)
    is_ctx = i < ncb
    lane = lax.broadcasted_iota(jnp.int32, (rows, LANES), 1)
    use_col = (lane & (DA_DH // 2)) != 0
    cos = jnp.where(use_col, csr_ref[...], csp_ref[pl.ds(grid_row, 1), :])
    sin = jnp.where(use_col, snr_ref[...], snp_ref[pl.ds(grid_row, 1), :])
    cos = jnp.where(is_ctx, 1.0, cos)
    sin = jnp.where(is_ctx, 0.0, sin)

    k = _rope(att[:, :W_DA], cos, sin)
    v = att[:, W_DA:2 * W_DA]
    q = _rope(att[:, 2 * W_DA:], cos, sin) * (DA_DH ** -0.5 * math.log2(math.e))
    slabs = W_DA // LANES
    _rows_to_batch(k, scr, 0, kq_ref, 0, BF16)
    _rows_to_batch(q, scr, slabs, kq_ref, W_DA, BF16)
    _rows_to_batch(v, scr, 2 * slabs, v_ref, 0, BF16)


def _in_proj(hs, mods, g, wf, wa, tables, *, n_ctx):
    n_rows, d = hs.shape
    tt = TOKEN_TILE
    assert tt == GRID_W
    tm = tt * BATCH
    ncb = n_ctx // tt
    s_len = n_rows // BATCH
    kind = lambda i: (i >= ncb).astype(jnp.int32)
    whole = lambda a: pl.BlockSpec(a.shape, lambda i: (0,) * a.ndim)
    return pl.pallas_call(
        functools.partial(_in_proj_kernel, ncb=ncb),
        grid=(n_rows // tm,),
        in_specs=[pl.BlockSpec((tm, d), lambda i: (i, 0)),
                  pl.BlockSpec((1, 3, BATCH, d), lambda i: (kind(i), 0, 0, 0)),
                  pl.BlockSpec((1, d), lambda i: (0, 0)),
                  pl.BlockSpec((d, PF_W), lambda i: (0, 0)),
                  pl.BlockSpec((d, PA_W), lambda i: (0, 0))] + [whole(t) for t in tables],
        out_specs=[pl.BlockSpec((tm, PF_W), lambda i: (i, 0)),
                   pl.BlockSpec((BATCH, tt, 2 * W_DA), lambda i: (0, i, 0)),
                   pl.BlockSpec((BATCH, tt, W_DA), lambda i: (0, i, 0))],
        out_shape=[jax.ShapeDtypeStruct((n_rows, PF_W), F32),
                   jax.ShapeDtypeStruct((BATCH, s_len, 2 * W_DA), BF16),
                   jax.ShapeDtypeStruct((BATCH, s_len, W_DA), BF16)],
        scratch_shapes=[pltpu.VMEM((PA_W // LANES, tm, LANES), F32)],
        compiler_params=_cparams(("arbitrary",)),
    )(hs, mods, g.reshape(1, d), wf, wa, *tables)


def _bwd_chunk(i, ncc, n):
    return jnp.where(i < ncc, ncc - 1 - i, n - 1 - i + ncc)


def _lru_kernel(ufc, ufp, ufn, ubc, ubp, ubn, cw_ref, cb_ref, wg_ref, bg_ref, lam_ref,
                hf_ref, hb_ref, a_scr, b_scr, carry_scr, *, ts, ncc, n):
    i = pl.program_id(0)

    @pl.when(i == 0)
    def _():
        carry_scr[...] = jnp.zeros_like(carry_scr)

    def prep(d, c, u_c, u_p, u_n):
        seg_start = jnp.logical_or(c == 0, c == ncc)
        seg_end = jnp.logical_or(c == ncc - 1, c == n - 1)
        prev = u_p[...] * jnp.where(seg_start, 0.0, 1.0)
        nxt = u_n[...] * jnp.where(seg_end, 0.0, 1.0)
        u = jnp.concatenate([prev, u_c[...], nxt], axis=0)
        cw = cw_ref[...]
        xc = cb_ref[...] + cw[0:1] * u[0:ts]
        for j in range(1, CONV_W):
            xc = xc + cw[j:j + 1] * u[j:j + ts]
        xc2 = xc.reshape(ts * xc.shape[1], W_LRU)
        gts = jnp.dot(xc2.astype(BF16), wg_ref[d], preferred_element_type=F32) + bg_ref[d]
        gate_r = jax.nn.sigmoid(gts[:, :W_LRU])
        gate_i = jax.nn.sigmoid(gts[:, W_LRU:])
        log_a = -LRU_C * gate_r * _softplus(-lam_ref[d])
        a = jnp.exp(log_a)
        th = jnp.tanh(log_a)
        one_minus_a2 = -2.0 * th / (1.0 - th)
        bb = jnp.sqrt(one_minus_a2) * (gate_i * xc2)
        a_scr[d] = a.reshape(xc.shape)
        b_scr[d] = bb.reshape(xc.shape)

    prep(0, i, ufc, ufp, ufn)
    prep(1, _bwd_chunk(i, ncc, n), ubc, ubp, ubn)

    def body(k, carry):
        hf, hb = carry
        kb = ts - 1 - k
        hf = a_scr[0, k] * hf + b_scr[0, k]
        hb = a_scr[1, kb] * hb + b_scr[1, kb]
        hf_ref[k] = hf
        hb_ref[kb] = hb
        return hf, hb

    hf, hb = lax.fori_loop(0, ts, body, (carry_scr[0], carry_scr[1]), unroll=8)
    carry_scr[0] = hf
    carry_scr[1] = hb


def _lru(pf3, conv_w, conv_b, wg, bg, lam, *, n_ctx):
    s_len, bsz, _ = pf3.shape
    ts = SCAN_TILE
    n = s_len // ts
    ncc = n_ctx // ts

    def cur(cfn):
        return pl.BlockSpec((ts, bsz, W_LRU), lambda i: (cfn(i), 0, 0))

    def prev(cfn):
        return pl.BlockSpec((2, bsz, W_LRU), lambda i: (jnp.maximum(cfn(i) * (ts // 2) - 1, 0), 0, 0))

    def nxt(cfn):
        return pl.BlockSpec((1, bsz, W_LRU), lambda i: (jnp.minimum((cfn(i) + 1) * ts, s_len - 1), 0, 0))

    fwd = lambda i: i
    bwd = lambda i: _bwd_chunk(i, ncc, n)
    whole = lambda shape: pl.BlockSpec(shape, lambda i: (0,) * len(shape))
    return pl.pallas_call(
        functools.partial(_lru_kernel, ts=ts, ncc=ncc, n=n),
        grid=(n,),
        in_specs=[cur(fwd), prev(fwd), nxt(fwd), cur(bwd), prev(bwd), nxt(bwd),
                  whole(conv_w.shape), whole(conv_b.shape), whole(wg.shape), whole(bg.shape),
                  whole(lam.shape)],
        out_specs=[pl.BlockSpec((ts, bsz, W_LRU), lambda i: (i, 0, 0)),
                   pl.BlockSpec((ts, bsz, W_LRU), lambda i: (bwd(i), 0, 0))],
        out_shape=[jax.ShapeDtypeStruct((s_len, bsz, W_LRU), F32)] * 2,
        scratch_shapes=[pltpu.VMEM((2, ts, bsz, W_LRU), F32),
                        pltpu.VMEM((2, ts, bsz, W_LRU), F32),
                        pltpu.VMEM((2, bsz, W_LRU), F32)],
        compiler_params=_cparams(("arbitrary",)),
    )(pf3, pf3, pf3, pf3, pf3, pf3, conv_w, conv_b, wg, bg, lam)


S5_LANE_CHUNK = 512


def _s5_kernel(uf_ref, ub_ref, lre_ref, lim_ref, ldt_ref, bre_ref, bim_ref, cre_ref, cim_ref,
               yf_ref, yb_ref, bbar_scr, a_scr, buf_scr, carry_scr, *, ts):
    i = pl.program_id(0)
    nst = S5_N
    bsz = uf_ref.shape[1]

    @pl.when(i == 0)
    def _():
        carry_scr[...] = jnp.zeros_like(carry_scr)
        for d in range(2):
            lre = lre_ref[d]
            lim = lim_ref[d]
            dt = jnp.exp(ldt_ref[d])
            mag = jnp.exp(lre * dt)
            ang = lim * dt
            ar = mag * jnp.cos(ang)
            ai = mag * jnp.sin(ang)
            nr = ar - 1.0
            den = lre * lre + lim * lim
            cfr = (nr * lre + ai * lim) / den
            cfi = (ai * lre - nr * lim) / den
            bre = bre_ref[d]
            bim = bim_ref[d]
            bbar_scr[d, :, :nst] = (cfr * bre - cfi * bim).astype(BF16)
            bbar_scr[d, :, nst:] = (cfr * bim + cfi * bre).astype(BF16)
            a_scr[d, 0] = jnp.broadcast_to(ar, (bsz, nst))
            a_scr[d, 1] = jnp.broadcast_to(ai, (bsz, nst))

    for d, u_ref in enumerate((uf_ref, ub_ref)):
        u2 = u_ref[...].reshape(ts * bsz, W_S5).astype(BF16)
        bu = jnp.dot(u2, bbar_scr[d], preferred_element_type=F32)
        buf_scr[d] = bu.reshape(ts, bsz, 2 * nst)

    lc = S5_LANE_CHUNK
    chains = [(d, pl.ds(c0, lc), pl.ds(nst + c0, lc)) for d in range(2) for c0 in range(0, nst, lc)]
    state = [(carry_scr[d, :, re_sl], carry_scr[d, :, im_sl]) for d, re_sl, im_sl in chains]
    for k in range(ts):
        for n_chain, (d, re_sl, im_sl) in enumerate(chains):
            t = k if d == 0 else ts - 1 - k
            ar = a_scr[d, 0, :, re_sl]
            ai = a_scr[d, 1, :, re_sl]
            sre, sim = state[n_chain]
            nre = ar * sre - ai * sim + buf_scr[d, t, :, re_sl]
            nim = ar * sim + ai * sre + buf_scr[d, t, :, im_sl]
            buf_scr[d, t, :, re_sl] = nre
            buf_scr[d, t, :, im_sl] = nim
            state[n_chain] = (nre, nim)
    for (d, re_sl, im_sl), (sre, sim) in zip(chains, state):
        carry_scr[d, :, re_sl] = sre
        carry_scr[d, :, im_sl] = sim

    for d, y_ref in enumerate((yf_ref, yb_ref)):
        st = buf_scr[d].reshape(ts * bsz, 2 * nst)
        y = (jnp.dot(st[:, :nst].astype(BF16), cre_ref[d], preferred_element_type=F32)
             - jnp.dot(st[:, nst:].astype(BF16), cim_ref[d], preferred_element_type=F32))
        y_ref[...] = y.reshape(ts, bsz, W_S5)


def _s5(pf3, lre, lim, ldt, bre, bim, cre, cim, *, n_ctx):
    s_len, bsz, _ = pf3.shape
    ts = SCAN_TILE
    n = s_len // ts
    ncc = n_ctx // ts
    bwd = lambda i: _bwd_chunk(i, ncc, n)
    col = W_LRU // W_S5
    whole = lambda a: pl.BlockSpec(a.shape, lambda i: (0,) * a.ndim)
    return pl.pallas_call(
        functools.partial(_s5_kernel, ts=ts),
        grid=(n,),
        in_specs=[pl.BlockSpec((ts, bsz, W_S5), lambda i: (i, 0, col)),
                  pl.BlockSpec((ts, bsz, W_S5), lambda i: (bwd(i), 0, col)),
                  whole(lre), whole(lim), whole(ldt), whole(bre), whole(bim), whole(cre), whole(cim)],
        out_specs=[pl.BlockSpec((ts, bsz, W_S5), lambda i: (i, 0, 0)),
                   pl.BlockSpec((ts, bsz, W_S5), lambda i: (bwd(i), 0, 0))],
        out_shape=[jax.ShapeDtypeStruct((s_len, bsz, W_S5), F32)] * 2,
        scratch_shapes=[pltpu.VMEM((2, W_S5, 2 * S5_N), BF16),
                        pltpu.VMEM((2, 2, bsz, S5_N), F32),
                        pltpu.VMEM((2, ts, bsz, 2 * S5_N), F32),
                        pltpu.VMEM((2, bsz, 2 * S5_N), F32)],
        compiler_params=_cparams(("arbitrary",)),
    )(pf3, pf3, lre, lim, ldt, bre, bim, cre, cim)


V_ROWS = DA_DV + 16


def _attn_kernel(q_ref, k_ref, v_ref, dl_ref, g_ref, o_ref, s_scr, vt_scr, *, tq, n_ctx, lam_init):
    s_len = k_ref.shape[1]
    nq = s_len // tq
    ncq = n_ctx // tq
    for c0 in range(0, s_len, tq):
        vt_scr[0:DA_DV, c0:c0 + tq] = v_ref[0, c0:c0 + tq, :].astype(F32).T.astype(BF16)
    pad_row = lax.broadcasted_iota(jnp.int32, (V_ROWS - DA_DV, s_len), 0)
    vt_scr[DA_DV:, :] = jnp.where(pad_row == 0, 1.0, 0.0).astype(BF16)
    dl = dl_ref[...]
    lam = (jnp.exp(jnp.sum(dl[0:1] * dl[1:2], axis=-1, keepdims=True))
           - jnp.exp(jnp.sum(dl[2:3] * dl[3:4], axis=-1, keepdims=True)) + lam_init)
    gscale = g_ref[...] * (1.0 - lam_init)
    lane = lax.broadcasted_iota(jnp.int32, (tq, LANES), 1)

    def rows(qt):
        return pl.ds(pl.multiple_of(qt * tq, tq), tq)

    def scores(qt, mp, nk):
        q = q_ref[0, rows(qt), :]
        qm = jnp.where((lane < DA_DH) if mp == 0 else (lane >= DA_DH), q, jnp.zeros_like(q))
        s_t = lax.dot_general(k_ref[0, 0:nk, :], qm, (((1,), (1,)), ((), ())),
                              preferred_element_type=F32)
        s_scr[mp, 0:nk, :] = s_t
        return jnp.max(s_t, axis=0, keepdims=True)

    def values(mp, m, nk):
        p_t = jnp.exp2(s_scr[mp, 0:nk, :] - m).astype(BF16)
        return jnp.dot(vt_scr[:, 0:nk], p_t, preferred_element_type=F32)

    def finish(qt, o1, o2):
        o_t = (o1[:DA_DV] * (1.0 / o1[DA_DV:DA_DV + 1])
               - o2[:DA_DV] * (lam / o2[DA_DV:DA_DV + 1]))
        y_t = o_t * lax.rsqrt(jnp.mean(o_t * o_t, axis=0, keepdims=True) + EPS)
        o_ref[0, rows(qt), :] = y_t.T * gscale

    for qt in range(ncq):
        m0 = scores(qt, 0, n_ctx)
        m1 = scores(qt, 1, n_ctx)
        finish(qt, values(0, m0, n_ctx), values(1, m1, n_ctx))

    def body(qt, m0):
        m1 = scores(qt, 1, s_len)
        o1 = values(0, m0, s_len)
        m0_next = scores(qt + 1, 0, s_len)
        o2 = values(1, m1, s_len)
        finish(qt, o1, o2)
        return m0_next

    m0 = lax.fori_loop(ncq, nq - 1, body, scores(ncq, 0, s_len))
    m1 = scores(nq - 1, 1, s_len)
    finish(nq - 1, values(0, m0, s_len), values(1, m1, s_len))


def _attention(qk, v, da_lam, da_g, *, n_ctx, lam_init):
    bsz, s_len, _ = qk.shape
    tq = Q_TILE
    assert n_ctx % tq == 0 and s_len // tq >= n_ctx // tq + 2
    return pl.pallas_call(
        functools.partial(_attn_kernel, tq=tq, n_ctx=n_ctx, lam_init=lam_init),
        grid=(bsz, DA_HEADS),
        in_specs=[pl.BlockSpec((1, s_len, LANES), lambda b, h: (b, 0, DA_HEADS + h)),
                  pl.BlockSpec((1, s_len, LANES), lambda b, h: (b, 0, h)),
                  pl.BlockSpec((1, s_len, LANES), lambda b, h: (b, 0, h)),
                  pl.BlockSpec((4, DA_DH), lambda b, h: (0, 0)),
                  pl.BlockSpec((1, DA_DV), lambda b, h: (0, 0))],
        out_specs=pl.BlockSpec((1, s_len, LANES), lambda b, h: (b, 0, h)),
        out_shape=jax.ShapeDtypeStruct((bsz, s_len, W_DA), F32),
        scratch_shapes=[pltpu.VMEM((2, s_len, tq), F32), pltpu.VMEM((V_ROWS, s_len), BF16)],
        compiler_params=_cparams(("arbitrary", "arbitrary")),
    )(qk, qk, v, da_lam.reshape(4, DA_DH), da_g.reshape(1, DA_DV))


def _gelu_tanh(x):
    return 0.5 * x * (1.0 + jnp.tanh(math.sqrt(2.0 / math.pi) * (x + 0.044715 * (x * x * x))))


def _out_proj_kernel(hf_ref, hb_ref, yf_ref, yb_ref, us_ref, ga_ref, gs_ref, gd_ref, yd_ref, h_ref,
                     mod_ref, d_ref, wglu_ref, bglu_ref, wo_ref, fg_ref, o_ref, scr, *, final):
    y_a = (hf_ref[...] + hb_ref[...]) * _silu(ga_ref[...])
    z = _gelu_tanh(d_ref[...] * us_ref[...] + yf_ref[...] + yb_ref[...])
    glu = jax.nn.sigmoid(jnp.dot(z.astype(BF16), wglu_ref[...], preferred_element_type=F32)
                         + bglu_ref[...])
    y_s = z * glu * _silu(gs_ref[...])
    y_d = _batch_to_rows(yd_ref, scr) * _silu(gd_ref[...])
    out = (jnp.dot(y_a.astype(BF16), wo_ref[0:W_LRU, :], preferred_element_type=F32)
           + jnp.dot(y_s.astype(BF16), wo_ref[W_LRU:W_LRU + W_S5, :], preferred_element_type=F32)
           + jnp.dot(y_d.astype(BF16), wo_ref[W_LRU + W_S5:, :], preferred_element_type=F32))
    rows, d = out.shape
    gate = mod_ref[0, 2]
    hn = h_ref[...] + (out.reshape(rows // BATCH, BATCH, d) * gate[None]).reshape(rows, d)
    if final:
        hn = hn * lax.rsqrt(jnp.mean(hn * hn, axis=-1, keepdims=True) + EPS) * fg_ref[...]
        _rows_to_batch(hn, scr, 0, o_ref, 0, F32)
    else:
        o_ref[...] = hn


def _out_proj(hf, hb, yf, yb, pf, yd, hs, mods, s5_d, w_glu, b_glu, w_out, final_g, *, n_ctx, final):
    n_rows, d = hs.shape
    tt = TOKEN_TILE
    tm = tt * BATCH
    ncb = n_ctx // tt
    off = ncb if final else 0
    nblk = n_rows // tm - off
    blk = lambda w, c=0: pl.BlockSpec((tm, w), lambda i: (i + off, c))
    whole = lambda a: pl.BlockSpec(a.shape, lambda i: (0,) * a.ndim)
    kind = lambda i: (i + off >= ncb).astype(jnp.int32)
    s5_d = s5_d.reshape(1, W_S5)
    b_glu = b_glu.reshape(1, W_S5)
    final_g = final_g.reshape(1, d)
    if final:
        out_spec = pl.BlockSpec((BATCH, tt, d), lambda i: (0, i, 0))
        out_shape = jax.ShapeDtypeStruct((BATCH, nblk * tt, d), F32)
    else:
        out_spec = pl.BlockSpec((tm, d), lambda i: (i, 0))
        out_shape = jax.ShapeDtypeStruct((nblk * tm, d), F32)
    return pl.pallas_call(
        functools.partial(_out_proj_kernel, final=final),
        grid=(nblk,),
        in_specs=[blk(W_LRU), blk(W_LRU), blk(W_S5), blk(W_S5),
                  blk(W_S5, 1), blk(W_LRU, 2), blk(W_S5, 3), blk(W_DA, 2),
                  pl.BlockSpec((BATCH, tt, W_DA), lambda i: (0, i + off, 0)), blk(d),
                  pl.BlockSpec((1, 3, BATCH, d), lambda i: (kind(i), 0, 0, 0)),
                  whole(s5_d), whole(w_glu), whole(b_glu), whole(w_out), whole(final_g)],
        out_specs=out_spec,
        out_shape=out_shape,
        scratch_shapes=[pltpu.VMEM((d // LANES, tm, LANES), F32)],
        compiler_params=_cparams(("arbitrary",)),
    )(hf, hb, yf, yb, pf, pf, pf, pf, yd, hs, mods, s5_d, w_glu, b_glu, w_out, final_g)


def _block_diag(w):
    nb, k, n = w.shape
    eye = jnp.eye(nb, dtype=w.dtype)
    return jnp.einsum('nkd,nm->nkmd', w, eye).reshape(nb * k, nb * n)


def kernel(x, c, ctx, c_ctx, norm_g, w_mod, b_mod, w_in, w_out, lru_conv_w, lru_conv_b, lru_wa, lru_ba,
           lru_wx, lru_bx, lru_lam, s5_lam_re, s5_lam_im, s5_log_dt, s5_b_re, s5_b_im, s5_c_re, s5_c_im,
           s5_d, s5_w_glu, s5_b_glu, da_lam, da_norm_g, final_g):
    bsz, t_len, d = x.shape
    n_ctx = ctx.shape[1]
    s_len = n_ctx + t_len
    n_rows = s_len * bsz
    n_layers = w_in.shape[0]
    assert bsz == BATCH and bsz + 1 <= MOD_ROWS
    assert n_ctx % Q_TILE == 0 and t_len % Q_TILE == 0
    assert n_ctx % TOKEN_TILE == 0 and n_ctx % SCAN_TILE == 0

    hs = _to_rows(ctx, x)

    cc = jnp.concatenate([c, c_ctx[None], jnp.zeros((MOD_ROWS - bsz - 1, d), F32)], axis=0)
    mod = _modulation(cc, w_mod.astype(BF16), b_mod).reshape(n_layers, MOD_ROWS, 3, d)
    mod_lat = mod[:, :bsz].transpose(0, 2, 1, 3)
    mod_ctx = jnp.broadcast_to(mod[:, bsz:bsz + 1], (n_layers, bsz, 3, d)).transpose(0, 2, 1, 3)
    mods = jnp.stack([mod_ctx, mod_lat], axis=1)

    tables = _rope_tables()

    side = W_LRU + W_S5 + 2 * W_DA
    c_ua, c_us = (0, W_LRU), (W_LRU, W_LRU + W_S5)
    c_k, c_v = (W_LRU + W_S5, W_LRU + W_S5 + W_DA), (W_LRU + W_S5 + W_DA, side)
    sl = lambda lo_hi, base=0: w_in[:, :, base + lo_hi[0]:base + lo_hi[1]]
    wf = jnp.concatenate([sl(c_ua), sl(c_us), sl(c_ua, side), sl(c_us, side), sl(c_v, side)],
                         axis=-1).astype(BF16)
    wa = jnp.concatenate([sl(c_k), sl(c_v), sl(c_k, side)], axis=-1).astype(BF16)
    w_out_b = w_out.astype(BF16)
    w_glu_b = s5_w_glu.astype(BF16)

    for l in range(n_layers):
        final = l == n_layers - 1
        lam_init = 0.8 - 0.6 * math.exp(-0.3 * l)
        pf, kq, v = _in_proj(hs, mods[l], norm_g[l], wf[l], wa[l], tables, n_ctx=n_ctx)
        pf3 = pf.reshape(s_len, bsz, PF_W)

        wg = jnp.stack([jnp.concatenate([_block_diag(lru_wa[l, dd]), _block_diag(lru_wx[l, dd])], axis=1)
                        for dd in range(2)]).astype(BF16)
        bg = jnp.concatenate([lru_ba[l], lru_bx[l]], axis=-1).reshape(2, 1, 2 * W_LRU)
        hf, hb = _lru(pf3, lru_conv_w[l], lru_conv_b[l].reshape(1, W_LRU), wg, bg,
                      lru_lam[l].reshape(2, 1, W_LRU), n_ctx=n_ctx)

        flat = lambda a: a.reshape(2, 1, S5_N)
        ldt = jnp.broadcast_to(s5_log_dt[l][:, :, None], (2, S5_G, S5_P))
        bre = jnp.stack([_block_diag(jnp.swapaxes(s5_b_re[l, dd], 1, 2)) for dd in range(2)])
        bim = jnp.stack([_block_diag(jnp.swapaxes(s5_b_im[l, dd], 1, 2)) for dd in range(2)])
        cre = jnp.stack([_block_diag(jnp.swapaxes(s5_c_re[l, dd], 1, 2)) for dd in range(2)]).astype(BF16)
        cim = jnp.stack([_block_diag(jnp.swapaxes(s5_c_im[l, dd], 1, 2)) for dd in range(2)]).astype(BF16)
        yf, yb = _s5(pf3, flat(s5_lam_re[l]), flat(s5_lam_im[l]), flat(ldt), bre, bim, cre, cim,
                     n_ctx=n_ctx)

        yd = _attention(kq, v, da_lam[l], da_norm_g[l], n_ctx=n_ctx, lam_init=lam_init)

        hs = _out_proj(hf.reshape(n_rows, W_LRU), hb.reshape(n_rows, W_LRU),
                       yf.reshape(n_rows, W_S5), yb.reshape(n_rows, W_S5),
                       pf, yd, hs, mods[l], s5_d[l], w_glu_b[l], s5_b_glu[l], w_out_b[l], final_g,
                       n_ctx=n_ctx, final=final)
    return hs
```

```python
import functools
import math

import jax
import jax.numpy as jnp
from jax import lax
from jax.experimental import pallas as pl
from jax.experimental.pallas import tpu as pltpu

F32 = jnp.float32
BF16 = jnp.bfloat16

GRID_W = 64
EPS = 1e-6
W_LRU = 256
LRU_BLOCKS = 4
CONV_W = 4
LRU_C = 8.0
W_S5 = 256
S5_H = 16
S5_G = 16
S5_P = 64
S5_N = S5_G * S5_P
DA_HEADS = 4
DA_DH = 64
DA_DV = 2 * DA_DH
W_DA = DA_HEADS * DA_DV
ROPE_BASE = 10000.0
ROPE_F = DA_DH // 4
LANES = 128
BATCH = 8

PF_W = 2 * W_LRU + 2 * W_S5 + W_DA
PA_W = 3 * W_DA

TOKEN_TILE = 64
SCAN_TILE = 64
Q_TILE = 256
MOD_ROWS = 16
VMEM_LIMIT = 48 * 1024 * 1024


def _cparams(sem):
    return pltpu.CompilerParams(dimension_semantics=sem, vmem_limit_bytes=VMEM_LIMIT)


def _silu(x):
    return x * jax.nn.sigmoid(x)


def _softplus(x):
    return jnp.maximum(x, 0.0) + jnp.log1p(jnp.exp(-jnp.abs(x)))


def _rows_to_batch(val, scr, slab0, out_ref, col0, dtype):
    rows, width = val.shape
    tt = rows // BATCH
    for j in range(width // LANES):
        scr[slab0 + j] = val[:, j * LANES:(j + 1) * LANES]
    for b in range(BATCH):
        for j in range(width // LANES):
            out_ref[b, :, col0 + j * LANES:col0 + (j + 1) * LANES] = (
                scr[slab0 + j, pl.ds(b, tt, stride=BATCH), :].astype(dtype))


def _batch_to_rows(in_ref, scr):
    _, tt, width = in_ref.shape
    for b in range(BATCH):
        for j in range(width // LANES):
            scr[j, pl.ds(b, tt, stride=BATCH), :] = in_ref[b, :, j * LANES:(j + 1) * LANES].astype(F32)
    return jnp.concatenate([scr[j] for j in range(width // LANES)], axis=1)


def _to_rows_kernel(ctx_ref, x_ref, o_ref, scr, *, ncb):
    i = pl.program_id(0)

    @pl.when(i < ncb)
    def _():
        o_ref[...] = _batch_to_rows(ctx_ref, scr)

    @pl.when(i >= ncb)
    def _():
        o_ref[...] = _batch_to_rows(x_ref, scr)


def _to_rows(ctx, x):
    bsz, n_ctx, d = ctx.shape
    t_len = x.shape[1]
    tt = TOKEN_TILE
    ncb = n_ctx // tt
    n = ncb + t_len // tt
    return pl.pallas_call(
        functools.partial(_to_rows_kernel, ncb=ncb),
        grid=(n,),
        in_specs=[pl.BlockSpec((bsz, tt, d), lambda i: (0, jnp.minimum(i, ncb - 1), 0)),
                  pl.BlockSpec((bsz, tt, d), lambda i: (0, jnp.maximum(i - ncb, 0), 0))],
        out_specs=pl.BlockSpec((tt * bsz, d), lambda i: (i, 0)),
        out_shape=jax.ShapeDtypeStruct((n * tt * bsz, d), F32),
        scratch_shapes=[pltpu.VMEM((d // LANES, tt * bsz, LANES), F32)],
        compiler_params=_cparams(("arbitrary",)),
    )(ctx, x)


def _rope_table_kernel(csp_ref, snp_ref, csr_ref, snr_ref):
    def table(shape, pos_shift):
        pos = lax.broadcasted_iota(jnp.int32, shape, 0) >> pos_shift
        lane = lax.broadcasted_iota(jnp.int32, shape, 1)
        f = (lane & (ROPE_F - 1)).astype(F32)
        ang = pos.astype(F32) * jnp.exp(-math.log(ROPE_BASE) * f / ROPE_F)
        sn = jnp.sin(ang)
        return jnp.cos(ang), jnp.where((lane & ROPE_F) == 0, -sn, sn)

    csp_ref[...], snp_ref[...] = table(csp_ref.shape, 0)
    csr_ref[...], snr_ref[...] = table(csr_ref.shape, int(math.log2(BATCH)))


def _rope_tables():
    shapes = [(GRID_W, LANES)] * 2 + [(GRID_W * BATCH, LANES)] * 2
    return pl.pallas_call(
        _rope_table_kernel,
        out_shape=[jax.ShapeDtypeStruct(s, F32) for s in shapes],
        compiler_params=pltpu.CompilerParams(vmem_limit_bytes=VMEM_LIMIT),
    )()


def _mod_kernel(c_ref, w_ref, b_ref, o_ref):
    c = c_ref[...]
    o_ref[0] = jnp.dot(_silu(c).astype(BF16), w_ref[0], preferred_element_type=F32) + b_ref[0]


def _modulation(cc, w_mod, b_mod):
    n_layers, d, d3 = w_mod.shape
    return pl.pallas_call(
        _mod_kernel,
        grid=(n_layers, d3 // d),
        in_specs=[pl.BlockSpec((MOD_ROWS, d), lambda l, j: (0, 0)),
                  pl.BlockSpec((1, d, d), lambda l, j: (l, 0, j)),
                  pl.BlockSpec((1, 1, d), lambda l, j: (l, 0, j))],
        out_specs=pl.BlockSpec((1, MOD_ROWS, d), lambda l, j: (l, 0, j)),
        out_shape=jax.ShapeDtypeStruct((n_layers, MOD_ROWS, d3), F32),
        compiler_params=_cparams(("arbitrary", "arbitrary")),
    )(cc, w_mod, b_mod.reshape(n_layers, 1, d3))


def _rope(x, cos, sin):
    width = x.shape[1]
    reps = width // LANES
    cos = jnp.concatenate([cos] * reps, axis=1)
    sin = jnp.concatenate([sin] * reps, axis=1)
    lane = lax.broadcasted_iota(jnp.int32, x.shape, 1)
    first = (lane & ROPE_F) == 0
    partner = jnp.where(first, pltpu.roll(x, width - ROPE_F, 1), pltpu.roll(x, ROPE_F, 1))
    return x * cos + partner * sin


def _in_proj_kernel(x_ref, mod_ref, g_ref, wf_ref, wa_ref, csp_ref, snp_ref, csr_ref, snr_ref,
                    pf_ref, kq_ref, v_ref, scr, *, ncb):
    i = pl.program_id(0)
    x = x_ref[...]
    rows, d = x.shape
    y = x * lax.rsqrt(jnp.mean(x * x, axis=-1, keepdims=True) + EPS) * g_ref[...]
    shift = mod_ref[0, 0]
    scale = mod_ref[0, 1]
    y3 = y.reshape(rows // BATCH, BATCH, d)
    n = (y3 * (1.0 + scale)[None] + shift[None]).reshape(rows, d).astype(BF16)
    pf_ref[...] = jnp.dot(n, wf_ref[...], preferred_element_type=F32)
    att = jnp.dot(n, wa_ref[...], preferred_element_type=F32)
    grid_row = jnp.maximum(i - ncb, 0)
    is_ctx = i < ncb
    lane = lax.broadcasted_iota(jnp.int32, (rows, LANES), 1)
    use_col = (lane & (DA_DH // 2)) != 0
    cos = jnp.where(use_col, csr_ref[...], csp_ref[pl.ds(grid_row, 1), :])
    sin = jnp.where(use_col, snr_ref[...], snp_ref[pl.ds(grid_row, 1), :])
    cos = jnp.where(is_ctx, 1.0, cos)
    sin = jnp.where(is_ctx, 0.0, sin)

    k = _rope(att[:, :W_DA], cos, sin)
    v = att[:, W_DA:2 * W_DA]
    q = _rope(att[:, 2 * W_DA:], cos, sin) * (DA_DH ** -0.5 * math.log2(math.e))
    slabs = W_DA // LANES
    _rows_to_batch(k, scr, 0, kq_ref, 0, BF16)
    _rows_to_batch(q, scr, slabs, kq_ref, W_DA, BF16)
    _rows_to_batch(v, scr, 2 * slabs, v_ref, 0, BF16)


def _in_proj(hs, mods, g, wf, wa, tables, *, n_ctx):
    n_rows, d = hs.shape
    tt = TOKEN_TILE
    assert tt == GRID_W
    tm = tt * BATCH
    ncb = n_ctx // tt
    s_len = n_rows // BATCH
    kind = lambda i: (i >= ncb).astype(jnp.int32)
    whole = lambda a: pl.BlockSpec(a.shape, lambda i: (0,) * a.ndim)
    return pl.pallas_call(
        functools.partial(_in_proj_kernel, ncb=ncb),
        grid=(n_rows // tm,),
        in_specs=[pl.BlockSpec((tm, d), lambda i: (i, 0)),
                  pl.BlockSpec((1, 3, BATCH, d), lambda i: (kind(i), 0, 0, 0)),
                  pl.BlockSpec((1, d), lambda i: (0, 0)),
                  pl.BlockSpec((d, PF_W), lambda i: (0, 0)),
                  pl.BlockSpec((d, PA_W), lambda i: (0, 0))] + [whole(t) for t in tables],
        out_specs=[pl.BlockSpec((tm, PF_W), lambda i: (i, 0)),
                   pl.BlockSpec((BATCH, tt, 2 * W_DA), lambda i: (0, i, 0)),
                   pl.BlockSpec((BATCH, tt, W_DA), lambda i: (0, i, 0))],
        out_shape=[jax.ShapeDtypeStruct((n_rows, PF_W), F32),
                   jax.ShapeDtypeStruct((BATCH, s_len, 2 * W_DA), BF16),
                   jax.ShapeDtypeStruct((BATCH, s_len, W_DA), BF16)],
        scratch_shapes=[pltpu.VMEM((PA_W // LANES, tm, LANES), F32)],
        compiler_params=_cparams(("arbitrary",)),
    )(hs, mods, g.reshape(1, d), wf, wa, *tables)


def _bwd_chunk(i, ncc, n):
    return jnp.where(i < ncc, ncc - 1 - i, n - 1 - i + ncc)


def _lru_kernel(ufc, ufp, ufn, ubc, ubp, ubn, cw_ref, cb_ref, wg_ref, bg_ref, lam_ref,
                hf_ref, hb_ref, a_scr, b_scr, carry_scr, *, ts, ncc, n):
    i = pl.program_id(0)

    @pl.when(i == 0)
    def _():
        carry_scr[...] = jnp.zeros_like(carry_scr)

    def prep(d, c, u_c, u_p, u_n):
        seg_start = jnp.logical_or(c == 0, c == ncc)
        seg_end = jnp.logical_or(c == ncc - 1, c == n - 1)
        prev = u_p[...] * jnp.where(seg_start, 0.0, 1.0)
        nxt = u_n[...] * jnp.where(seg_end, 0.0, 1.0)
        u = jnp.concatenate([prev, u_c[...], nxt], axis=0)
        cw = cw_ref[...]
        xc = cb_ref[...] + cw[0:1] * u[0:ts]
        for j in range(1, CONV_W):
            xc = xc + cw[j:j + 1] * u[j:j + ts]
        xc2 = xc.reshape(ts * xc.shape[1], W_LRU)
        gts = jnp.dot(xc2.astype(BF16), wg_ref[d], preferred_element_type=F32) + bg_ref[d]
        gate_r = jax.nn.sigmoid(gts[:, :W_LRU])
        gate_i = jax.nn.sigmoid(gts[:, W_LRU:])
        log_a = -LRU_C * gate_r * _softplus(-lam_ref[d])
        a = jnp.exp(log_a)
        th = jnp.tanh(log_a)
        one_minus_a2 = -2.0 * th / (1.0 - th)
        bb = jnp.sqrt(one_minus_a2) * (gate_i * xc2)
        a_scr[d] = a.reshape(xc.shape)
        b_scr[d] = bb.reshape(xc.shape)

    prep(0, i, ufc, ufp, ufn)
    prep(1, _bwd_chunk(i, ncc, n), ubc, ubp, ubn)

    def body(k, carry):
        hf, hb = carry
        kb = ts - 1 - k
        hf = a_scr[0, k] * hf + b_scr[0, k]
        hb = a_scr[1, kb] * hb + b_scr[1, kb]
        hf_ref[k] = hf
        hb_ref[kb] = hb
        return hf, hb

    hf, hb = lax.fori_loop(0, ts, body, (carry_scr[0], carry_scr[1]), unroll=8)
    carry_scr[0] = hf
    carry_scr[1] = hb


def _lru(pf3, conv_w, conv_b, wg, bg, lam, *, n_ctx):
    s_len, bsz, _ = pf3.shape
    ts = SCAN_TILE
    n = s_len // ts
    ncc = n_ctx // ts

    def cur(cfn):
        return pl.BlockSpec((ts, bsz, W_LRU), lambda i: (cfn(i), 0, 0))

    def prev(cfn):
        return pl.BlockSpec((2, bsz, W_LRU), lambda i: (jnp.maximum(cfn(i) * (ts // 2) - 1, 0), 0, 0))

    def nxt(cfn):
        return pl.BlockSpec((1, bsz, W_LRU), lambda i: (jnp.minimum((cfn(i) + 1) * ts, s_len - 1), 0, 0))

    fwd = lambda i: i
    bwd = lambda i: _bwd_chunk(i, ncc, n)
    whole = lambda shape: pl.BlockSpec(shape, lambda i: (0,) * len(shape))
    return pl.pallas_call(
        functools.partial(_lru_kernel, ts=ts, ncc=ncc, n=n),
        grid=(n,),
        in_specs=[cur(fwd), prev(fwd), nxt(fwd), cur(bwd), prev(bwd), nxt(bwd),
                  whole(conv_w.shape), whole(conv_b.shape), whole(wg.shape), whole(bg.shape),
                  whole(lam.shape)],
        out_specs=[pl.BlockSpec((ts, bsz, W_LRU), lambda i: (i, 0, 0)),
                   pl.BlockSpec((ts, bsz, W_LRU), lambda i: (bwd(i), 0, 0))],
        out_shape=[jax.ShapeDtypeStruct((s_len, bsz, W_LRU), F32)] * 2,
        scratch_shapes=[pltpu.VMEM((2, ts, bsz, W_LRU), F32),
                        pltpu.VMEM((2, ts, bsz, W_LRU), F32),
                        pltpu.VMEM((2, bsz, W_LRU), F32)],
        compiler_params=_cparams(("arbitrary",)),
    )(pf3, pf3, pf3, pf3, pf3, pf3, conv_w, conv_b, wg, bg, lam)


S5_LANE_CHUNK = 512


def _s5_kernel(uf_ref, ub_ref, lre_ref, lim_ref, ldt_ref, bre_ref, bim_ref, cre_ref, cim_ref,
               yf_ref, yb_ref, bbar_scr, a_scr, buf_scr, carry_scr, *, ts):
    i = pl.program_id(0)
    nst = S5_N
    bsz = uf_ref.shape[1]

    @pl.when(i == 0)
    def _():
        carry_scr[...] = jnp.zeros_like(carry_scr)
        for d in range(2):
            lre = lre_ref[d]
            lim = lim_ref[d]
            dt = jnp.exp(ldt_ref[d])
            mag = jnp.exp(lre * dt)
            ang = lim * dt
            ar = mag * jnp.cos(ang)
            ai = mag * jnp.sin(ang)
            nr = ar - 1.0
            den = lre * lre + lim * lim
            cfr = (nr * lre + ai * lim) / den
            cfi = (ai * lre - nr * lim) / den
            bre = bre_ref[d]
            bim = bim_ref[d]
            bbar_scr[d, :, :nst] = (cfr * bre - cfi * bim).astype(BF16)
            bbar_scr[d, :, nst:] = (cfr * bim + cfi * bre).astype(BF16)
            a_scr[d, 0] = jnp.broadcast_to(ar, (bsz, nst))
            a_scr[d, 1] = jnp.broadcast_to(ai, (bsz, nst))

    for d, u_ref in enumerate((uf_ref, ub_ref)):
        u2 = u_ref[...].reshape(ts * bsz, W_S5).astype(BF16)
        bu = jnp.dot(u2, bbar_scr[d], preferred_element_type=F32)
        buf_scr[d] = bu.reshape(ts, bsz, 2 * nst)

    lc = S5_LANE_CHUNK
    chains = [(d, pl.ds(c0, lc), pl.ds(nst + c0, lc)) for d in range(2) for c0 in range(0, nst, lc)]
    state = [(carry_scr[d, :, re_sl], carry_scr[d, :, im_sl]) for d, re_sl, im_sl in chains]
    for k in range(ts):
        for n_chain, (d, re_sl, im_sl) in enumerate(chains):
            t = k if d == 0 else ts - 1 - k
            ar = a_scr[d, 0, :, re_sl]
            ai = a_scr[d, 1, :, re_sl]
            sre, sim = state[n_chain]
            nre = ar * sre - ai * sim + buf_scr[d, t, :, re_sl]
            nim = ar * sim + ai * sre + buf_scr[d, t, :, im_sl]
            buf_scr[d, t, :, re_sl] = nre
            buf_scr[d, t, :, im_sl] = nim
            state[n_chain] = (nre, nim)
    for (d, re_sl, im_sl), (sre, sim) in zip(chains, state):
        carry_scr[d, :, re_sl] = sre
        carry_scr[d, :, im_sl] = sim

    for d, y_ref in enumerate((yf_ref, yb_ref)):
        st = buf_scr[d].reshape(ts * bsz, 2 * nst)
        y = (jnp.dot(st[:, :nst].astype(BF16), cre_ref[d], preferred_element_type=F32)
             - jnp.dot(st[:, nst:].astype(BF16), cim_ref[d], preferred_element_type=F32))
        y_ref[...] = y.reshape(ts, bsz, W_S5)


def _s5(pf3, lre, lim, ldt, bre, bim, cre, cim, *, n_ctx):
    s_len, bsz, _ = pf3.shape
    ts = SCAN_TILE
    n = s_len // ts
    ncc = n_ctx // ts
    bwd = lambda i: _bwd_chunk(i, ncc, n)
    col = W_LRU // W_S5
    whole = lambda a: pl.BlockSpec(a.shape, lambda i: (0,) * a.ndim)
    return pl.pallas_call(
        functools.partial(_s5_kernel, ts=ts),
        grid=(n,),
        in_specs=[pl.BlockSpec((ts, bsz, W_S5), lambda i: (i, 0, col)),
                  pl.BlockSpec((ts, bsz, W_S5), lambda i: (bwd(i), 0, col)),
                  whole(lre), whole(lim), whole(ldt), whole(bre), whole(bim), whole(cre), whole(cim)],
        out_specs=[pl.BlockSpec((ts, bsz, W_S5), lambda i: (i, 0, 0)),
                   pl.BlockSpec((ts, bsz, W_S5), lambda i: (bwd(i), 0, 0))],
        out_shape=[jax.ShapeDtypeStruct((s_len, bsz, W_S5), F32)] * 2,
        scratch_shapes=[pltpu.VMEM((2, W_S5, 2 * S5_N), BF16),
                        pltpu.VMEM((2, 2, bsz, S5_N), F32),
                        pltpu.VMEM((2, ts, bsz, 2 * S5_N), F32),
                        pltpu.VMEM((2, bsz, 2 * S5_N), F32)],
        compiler_params=_cparams(("arbitrary",)),
    )(pf3, pf3, lre, lim, ldt, bre, bim, cre, cim)


V_ROWS = DA_DV + 16
SCORE_PARTS = 2
SCORE_SLOTS = 4


def _attn_kernel(q_ref, k_ref, v_ref, dl_ref, g_ref, o_ref, s_scr, vt_scr, *, tq, n_ctx, lam_init):
    s_len = k_ref.shape[1]
    nq = s_len // tq
    ncq = n_ctx // tq
    for c0 in range(0, s_len, tq):
        vt_scr[0:DA_DV, c0:c0 + tq] = v_ref[0, c0:c0 + tq, :].astype(F32).T.astype(BF16)
    pad_row = lax.broadcasted_iota(jnp.int32, (V_ROWS - DA_DV, s_len), 0)
    vt_scr[DA_DV:, :] = jnp.where(pad_row == 0, 1.0, 0.0).astype(BF16)
    dl = dl_ref[...]
    lam = (jnp.exp(jnp.sum(dl[0:1] * dl[1:2], axis=-1, keepdims=True))
           - jnp.exp(jnp.sum(dl[2:3] * dl[3:4], axis=-1, keepdims=True)) + lam_init)
    gscale = g_ref[...] * (1.0 - lam_init)
    lane = lax.broadcasted_iota(jnp.int32, (tq, LANES), 1)

    def rows(qt):
        return pl.ds(pl.multiple_of(qt * tq, tq), tq)

    def key_parts(nk, max_parts):
        groups = nk // LANES
        n_parts = min(max_parts, groups)
        bounds = [LANES * ((groups * p) // n_parts) for p in range(n_parts + 1)]
        return tuple(zip(bounds[:-1], bounds[1:]))

    def scores(qt, mp, nk, slot):
        q = q_ref[0, rows(qt), :]
        qm = jnp.where((lane < DA_DH) if mp == 0 else (lane >= DA_DH), q, jnp.zeros_like(q))
        m = None
        for lo, hi in key_parts(nk, SCORE_PARTS):
            s_t = lax.dot_general(k_ref[0, lo:hi, :], qm, (((1,), (1,)), ((), ())),
                                  preferred_element_type=F32)
            s_scr[slot, lo:hi, :] = s_t
            m_part = jnp.max(s_t, axis=0, keepdims=True)
            m = m_part if m is None else jnp.maximum(m, m_part)
        return m

    def values(slot, m, nk):
        p_t = jnp.exp2(s_scr[slot, 0:nk, :] - m).astype(BF16)
        return jnp.dot(vt_scr[:, 0:nk], p_t, preferred_element_type=F32)

    def finish(qt, o1, o2):
        o_t = (o1[:DA_DV] * (1.0 / o1[DA_DV:DA_DV + 1])
               - o2[:DA_DV] * (lam / o2[DA_DV:DA_DV + 1]))
        y_t = o_t * lax.rsqrt(jnp.mean(o_t * o_t, axis=0, keepdims=True) + EPS)
        o_ref[0, rows(qt), :] = y_t.T * gscale

    for qt in range(ncq):
        m0 = scores(qt, 0, n_ctx, 0)
        m1 = scores(qt, 1, n_ctx, 2)
        finish(qt, values(0, m0, n_ctx), values(2, m1, n_ctx))

    def body(j, m0):
        t0 = ncq + 2 * j
        m1 = scores(t0, 1, s_len, 2)
        o1 = values(0, m0, s_len)
        m0b = scores(t0 + 1, 0, s_len, 1)
        o2 = values(2, m1, s_len)
        finish(t0, o1, o2)
        m1b = scores(t0 + 1, 1, s_len, 3)
        o1b = values(1, m0b, s_len)
        m0_next = scores(jnp.minimum(t0 + 2, nq - 1), 0, s_len, 0)
        o2b = values(3, m1b, s_len)
        finish(t0 + 1, o1b, o2b)
        return m0_next

    lax.fori_loop(0, (nq - ncq) // 2, body, scores(ncq, 0, s_len, 0))


def _attention(qk, v, da_lam, da_g, *, n_ctx, lam_init):
    bsz, s_len, _ = qk.shape
    tq = Q_TILE
    assert n_ctx % tq == 0 and (s_len - n_ctx) % (2 * tq) == 0
    return pl.pallas_call(
        functools.partial(_attn_kernel, tq=tq, n_ctx=n_ctx, lam_init=lam_init),
        grid=(bsz, DA_HEADS),
        in_specs=[pl.BlockSpec((1, s_len, LANES), lambda b, h: (b, 0, DA_HEADS + h)),
                  pl.BlockSpec((1, s_len, LANES), lambda b, h: (b, 0, h)),
                  pl.BlockSpec((1, s_len, LANES), lambda b, h: (b, 0, h)),
                  pl.BlockSpec((4, DA_DH), lambda b, h: (0, 0)),
                  pl.BlockSpec((1, DA_DV), lambda b, h: (0, 0))],
        out_specs=pl.BlockSpec((1, s_len, LANES), lambda b, h: (b, 0, h)),
        out_shape=jax.ShapeDtypeStruct((bsz, s_len, W_DA), F32),
        scratch_shapes=[pltpu.VMEM((SCORE_SLOTS, s_len, tq), F32),
                        pltpu.VMEM((V_ROWS, s_len), BF16)],
        compiler_params=_cparams(("arbitrary", "arbitrary")),
    )(qk, qk, v, da_lam.reshape(4, DA_DH), da_g.reshape(1, DA_DV))


def _gelu_tanh(x):
    return 0.5 * x * (1.0 + jnp.tanh(math.sqrt(2.0 / math.pi) * (x + 0.044715 * (x * x * x))))


def _out_proj_kernel(hf_ref, hb_ref, yf_ref, yb_ref, us_ref, ga_ref, gs_ref, gd_ref, yd_ref, h_ref,
                     mod_ref, d_ref, wglu_ref, bglu_ref, wo_ref, fg_ref, o_ref, scr, *, final):
    y_a = (hf_ref[...] + hb_ref[...]) * _silu(ga_ref[...])
    z = _gelu_tanh(d_ref[...] * us_ref[...] + yf_ref[...] + yb_ref[...])
    glu = jax.nn.sigmoid(jnp.dot(z.astype(BF16), wglu_ref[...], preferred_element_type=F32)
                         + bglu_ref[...])
    y_s = z * glu * _silu(gs_ref[...])
    y_d = _batch_to_rows(yd_ref, scr) * _silu(gd_ref[...])
    out = (jnp.dot(y_a.astype(BF16), wo_ref[0:W_LRU, :], preferred_element_type=F32)
           + jnp.dot(y_s.astype(BF16), wo_ref[W_LRU:W_LRU + W_S5, :], preferred_element_type=F32)
           + jnp.dot(y_d.astype(BF16), wo_ref[W_LRU + W_S5:, :], preferred_element_type=F32))
    rows, d = out.shape
    gate = mod_ref[0, 2]
    hn = h_ref[...] + (out.reshape(rows // BATCH, BATCH, d) * gate[None]).reshape(rows, d)
    if final:
        hn = hn * lax.rsqrt(jnp.mean(hn * hn, axis=-1, keepdims=True) + EPS) * fg_ref[...]
        _rows_to_batch(hn, scr, 0, o_ref, 0, F32)
    else:
        o_ref[...] = hn


def _out_proj(hf, hb, yf, yb, pf, yd, hs, mods, s5_d, w_glu, b_glu, w_out, final_g, *, n_ctx, final):
    n_rows, d = hs.shape
    tt = TOKEN_TILE
    tm = tt * BATCH
    ncb = n_ctx // tt
    off = ncb if final else 0
    nblk = n_rows // tm - off
    blk = lambda w, c=0: pl.BlockSpec((tm, w), lambda i: (i + off, c))
    whole = lambda a: pl.BlockSpec(a.shape, lambda i: (0,) * a.ndim)
    kind = lambda i: (i + off >= ncb).astype(jnp.int32)
    s5_d = s5_d.reshape(1, W_S5)
    b_glu = b_glu.reshape(1, W_S5)
    final_g = final_g.reshape(1, d)
    if final:
        out_spec = pl.BlockSpec((BATCH, tt, d), lambda i: (0, i, 0))
        out_shape = jax.ShapeDtypeStruct((BATCH, nblk * tt, d), F32)
    else:
        out_spec = pl.BlockSpec((tm, d), lambda i: (i, 0))
        out_shape = jax.ShapeDtypeStruct((nblk * tm, d), F32)
    return pl.pallas_call(
        functools.partial(_out_proj_kernel, final=final),
        grid=(nblk,),
        in_specs=[blk(W_LRU), blk(W_LRU), blk(W_S5), blk(W_S5),
                  blk(W_S5, 1), blk(W_LRU, 2), blk(W_S5, 3), blk(W_DA, 2),
                  pl.BlockSpec((BATCH, tt, W_DA), lambda i: (0, i + off, 0)), blk(d),
                  pl.BlockSpec((1, 3, BATCH, d), lambda i: (kind(i), 0, 0, 0)),
                  whole(s5_d), whole(w_glu), whole(b_glu), whole(w_out), whole(final_g)],
        out_specs=out_spec,
        out_shape=out_shape,
        scratch_shapes=[pltpu.VMEM((d // LANES, tm, LANES), F32)],
        compiler_params=_cparams(("arbitrary",)),
    )(hf, hb, yf, yb, pf, pf, pf, pf, yd, hs, mods, s5_d, w_glu, b_glu, w_out, final_g)


def _block_diag(w):
    nb, k, n = w.shape
    eye = jnp.eye(nb, dtype=w.dtype)
    return jnp.einsum('nkd,nm->nkmd', w, eye).reshape(nb * k, nb * n)


def kernel(x, c, ctx, c_ctx, norm_g, w_mod, b_mod, w_in, w_out, lru_conv_w, lru_conv_b, lru_wa, lru_ba,
           lru_wx, lru_bx, lru_lam, s5_lam_re, s5_lam_im, s5_log_dt, s5_b_re, s5_b_im, s5_c_re, s5_c_im,
           s5_d, s5_w_glu, s5_b_glu, da_lam, da_norm_g, final_g):
    bsz, t_len, d = x.shape
    n_ctx = ctx.shape[1]
    s_len = n_ctx + t_len
    n_rows = s_len * bsz
    n_layers = w_in.shape[0]
    assert bsz == BATCH and bsz + 1 <= MOD_ROWS
    assert n_ctx % Q_TILE == 0 and t_len % Q_TILE == 0
    assert n_ctx % TOKEN_TILE == 0 and n_ctx % SCAN_TILE == 0

    hs = _to_rows(ctx, x)

    cc = jnp.concatenate([c, c_ctx[None], jnp.zeros((MOD_ROWS - bsz - 1, d), F32)], axis=0)
    mod = _modulation(cc, w_mod.astype(BF16), b_mod).reshape(n_layers, MOD_ROWS, 3, d)
    mod_lat = mod[:, :bsz].transpose(0, 2, 1, 3)
    mod_ctx = jnp.broadcast_to(mod[:, bsz:bsz + 1], (n_layers, bsz, 3, d)).transpose(0, 2, 1, 3)
    mods = jnp.stack([mod_ctx, mod_lat], axis=1)

    tables = _rope_tables()

    side = W_LRU + W_S5 + 2 * W_DA
    c_ua, c_us = (0, W_LRU), (W_LRU, W_LRU + W_S5)
    c_k, c_v = (W_LRU + W_S5, W_LRU + W_S5 + W_DA), (W_LRU + W_S5 + W_DA, side)
    sl = lambda lo_hi, base=0: w_in[:, :, base + lo_hi[0]:base + lo_hi[1]]
    wf = jnp.concatenate([sl(c_ua), sl(c_us), sl(c_ua, side), sl(c_us, side), sl(c_v, side)],
                         axis=-1).astype(BF16)
    wa = jnp.concatenate([sl(c_k), sl(c_v), sl(c_k, side)], axis=-1).astype(BF16)
    w_out_b = w_out.astype(BF16)
    w_glu_b = s5_w_glu.astype(BF16)

    for l in range(n_layers):
        final = l == n_layers - 1
        lam_init = 0.8 - 0.6 * math.exp(-0.3 * l)
        pf, kq, v = _in_proj(hs, mods[l], norm_g[l], wf[l], wa[l], tables, n_ctx=n_ctx)
        pf3 = pf.reshape(s_len, bsz, PF_W)

        wg = jnp.stack([jnp.concatenate([_block_diag(lru_wa[l, dd]), _block_diag(lru_wx[l, dd])], axis=1)
                        for dd in range(2)]).astype(BF16)
        bg = jnp.concatenate([lru_ba[l], lru_bx[l]], axis=-1).reshape(2, 1, 2 * W_LRU)
        hf, hb = _lru(pf3, lru_conv_w[l], lru_conv_b[l].reshape(1, W_LRU), wg, bg,
                      lru_lam[l].reshape(2, 1, W_LRU), n_ctx=n_ctx)

        flat = lambda a: a.reshape(2, 1, S5_N)
        ldt = jnp.broadcast_to(s5_log_dt[l][:, :, None], (2, S5_G, S5_P))
        bre = jnp.stack([_block_diag(jnp.swapaxes(s5_b_re[l, dd], 1, 2)) for dd in range(2)])
        bim = jnp.stack([_block_diag(jnp.swapaxes(s5_b_im[l, dd], 1, 2)) for dd in range(2)])
        cre = jnp.stack([_block_diag(jnp.swapaxes(s5_c_re[l, dd], 1, 2)) for dd in range(2)]).astype(BF16)
        cim = jnp.stack([_block_diag(jnp.swapaxes(s5_c_im[l, dd], 1, 2)) for dd in range(2)]).astype(BF16)
        yf, yb = _s5(pf3, flat(s5_lam_re[l]), flat(s5_lam_im[l]), flat(ldt), bre, bim, cre, cim,
                     n_ctx=n_ctx)

        yd = _attention(kq, v, da_lam[l], da_norm_g[l], n_ctx=n_ctx, lam_init=lam_init)

        hs = _out_proj(hf.reshape(n_rows, W_LRU), hb.reshape(n_rows, W_LRU),
                       yf.reshape(n_rows, W_S5), yb.reshape(n_rows, W_S5),
                       pf, yd, hs, mods[l], s5_d[l], w_glu_b[l], s5_b_glu[l], w_out_b[l], final_g,
                       n_ctx=n_ctx, final=final)
    return hs
```

```python
import functools
import math

import jax
import jax.numpy as jnp
from jax import lax
from jax.experimental import pallas as pl
from jax.experimental.pallas import tpu as pltpu

F32 = jnp.float32
BF16 = jnp.bfloat16

GRID_W = 64
EPS = 1e-6
W_LRU = 256
LRU_BLOCKS = 4
CONV_W = 4
LRU_C = 8.0
W_S5 = 256
S5_H = 16
S5_G = 16
S5_P = 64
S5_N = S5_G * S5_P
DA_HEADS = 4
DA_DH = 64
DA_DV = 2 * DA_DH
W_DA = DA_HEADS * DA_DV
ROPE_BASE = 10000.0
ROPE_F = DA_DH // 4
LANES = 128
BATCH = 8

PF_W = 2 * W_LRU + 2 * W_S5 + W_DA
PA_W = 3 * W_DA

TOKEN_TILE = 64
SCAN_TILE = 64
Q_TILE = 256
MOD_ROWS = 16
VMEM_LIMIT = 48 * 1024 * 1024


def _cparams(sem):
    return pltpu.CompilerParams(dimension_semantics=sem, vmem_limit_bytes=VMEM_LIMIT)


def _silu(x):
    return x * jax.nn.sigmoid(x)


def _softplus(x):
    return jnp.maximum(x, 0.0) + jnp.log1p(jnp.exp(-jnp.abs(x)))


def _rows_to_batch(val, scr, slab0, out_ref, col0, dtype):
    rows, width = val.shape
    tt = rows // BATCH
    for j in range(width // LANES):
        scr[slab0 + j] = val[:, j * LANES:(j + 1) * LANES]
    for b in range(BATCH):
        for j in range(width // LANES):
            out_ref[b, :, col0 + j * LANES:col0 + (j + 1) * LANES] = (
                scr[slab0 + j, pl.ds(b, tt, stride=BATCH), :].astype(dtype))


def _batch_to_rows(in_ref, scr):
    _, tt, width = in_ref.shape
    for b in range(BATCH):
        for j in range(width // LANES):
            scr[j, pl.ds(b, tt, stride=BATCH), :] = in_ref[b, :, j * LANES:(j + 1) * LANES].astype(F32)
    return jnp.concatenate([scr[j] for j in range(width // LANES)], axis=1)


def _rope_table_kernel(csp_ref, snp_ref, csr_ref, snr_ref):
    def table(shape, pos_shift):
        pos = lax.broadcasted_iota(jnp.int32, shape, 0) >> pos_shift
        lane = lax.broadcasted_iota(jnp.int32, shape, 1)
        f = (lane & (ROPE_F - 1)).astype(F32)
        ang = pos.astype(F32) * jnp.exp(-math.log(ROPE_BASE) * f / ROPE_F)
        sn = jnp.sin(ang)
        return jnp.cos(ang), jnp.where((lane & ROPE_F) == 0, -sn, sn)

    csp_ref[...], snp_ref[...] = table(csp_ref.shape, 0)
    csr_ref[...], snr_ref[...] = table(csr_ref.shape, int(math.log2(BATCH)))


def _rope_tables():
    shapes = [(GRID_W, LANES)] * 2 + [(GRID_W * BATCH, LANES)] * 2
    return pl.pallas_call(
        _rope_table_kernel,
        out_shape=[jax.ShapeDtypeStruct(s, F32) for s in shapes],
        compiler_params=pltpu.CompilerParams(vmem_limit_bytes=VMEM_LIMIT),
    )()


def _mod_kernel(c_ref, w_ref, b_ref, o_ref):
    c = c_ref[...]
    o_ref[0] = jnp.dot(_silu(c).astype(BF16), w_ref[0], preferred_element_type=F32) + b_ref[0]


def _modulation(cc, w_mod, b_mod):
    n_layers, d, d3 = w_mod.shape
    return pl.pallas_call(
        _mod_kernel,
        grid=(n_layers, d3 // d),
        in_specs=[pl.BlockSpec((MOD_ROWS, d), lambda l, j: (0, 0)),
                  pl.BlockSpec((1, d, d), lambda l, j: (l, 0, j)),
                  pl.BlockSpec((1, 1, d), lambda l, j: (l, 0, j))],
        out_specs=pl.BlockSpec((1, MOD_ROWS, d), lambda l, j: (l, 0, j)),
        out_shape=jax.ShapeDtypeStruct((n_layers, MOD_ROWS, d3), F32),
        compiler_params=_cparams(("arbitrary", "arbitrary")),
    )(cc, w_mod, b_mod.reshape(n_layers, 1, d3))


def _rope(x, cos, sin):
    width = x.shape[1]
    reps = width // LANES
    cos = jnp.concatenate([cos] * reps, axis=1)
    sin = jnp.concatenate([sin] * reps, axis=1)
    lane = lax.broadcasted_iota(jnp.int32, x.shape, 1)
    first = (lane & ROPE_F) == 0
    partner = jnp.where(first, pltpu.roll(x, width - ROPE_F, 1), pltpu.roll(x, ROPE_F, 1))
    return x * cos + partner * sin


def _project(i, x, mod_ref, g_ref, wf_ref, wa_ref, csp_ref, snp_ref, csr_ref, snr_ref,
             pf_ref, kq_ref, v_ref, scr, ncb):
    rows, d = x.shape
    y = x * lax.rsqrt(jnp.mean(x * x, axis=-1, keepdims=True) + EPS) * g_ref[...]
    shift = mod_ref[0, 0]
    scale = mod_ref[0, 1]
    y3 = y.reshape(rows // BATCH, BATCH, d)
    n = (y3 * (1.0 + scale)[None] + shift[None]).reshape(rows, d).astype(BF16)
    pf_ref[...] = jnp.dot(n, wf_ref[...], preferred_element_type=F32)
    att = jnp.dot(n, wa_ref[...], preferred_element_type=F32)
    grid_row = jnp.maximum(i - ncb, 0)
    is_ctx = i < ncb
    lane = lax.broadcasted_iota(jnp.int32, (rows, LANES), 1)
    use_col = (lane & (DA_DH // 2)) != 0
    cos = jnp.where(use_col, csr_ref[...], csp_ref[pl.ds(grid_row, 1), :])
    sin = jnp.where(use_col, snr_ref[...], snp_ref[pl.ds(grid_row, 1), :])
    cos = jnp.where(is_ctx, 1.0, cos)
    sin = jnp.where(is_ctx, 0.0, sin)

    k = _rope(att[:, :W_DA], cos, sin)
    v = att[:, W_DA:2 * W_DA]
    q = _rope(att[:, 2 * W_DA:], cos, sin) * (DA_DH ** -0.5 * math.log2(math.e))
    slabs = W_DA // LANES
    _rows_to_batch(k, scr, 0, kq_ref, 0, BF16)
    _rows_to_batch(q, scr, slabs, kq_ref, W_DA, BF16)
    _rows_to_batch(v, scr, 2 * slabs, v_ref, 0, BF16)


def _in_proj_kernel(ctx_ref, x_ref, *refs, ncb):
    proj_in, (hs_ref, pf_ref, kq_ref, v_ref, scr, rows_scr) = refs[:8], refs[8:]
    i = pl.program_id(0)

    @pl.when(i < ncb)
    def _():
        hs_ref[...] = _batch_to_rows(ctx_ref, rows_scr)

    @pl.when(i >= ncb)
    def _():
        hs_ref[...] = _batch_to_rows(x_ref, rows_scr)

    _project(i, hs_ref[...], *proj_in, pf_ref, kq_ref, v_ref, scr, ncb)


def _project_specs(d, tt, n_rows, kind, tables):
    tm = tt * BATCH
    s_len = n_rows // BATCH
    const = lambda shape: pl.BlockSpec(shape, lambda i: (0,) * len(shape),
                                       pipeline_mode=pl.Buffered(1))
    in_specs = [pl.BlockSpec((1, 3, BATCH, d), lambda i: (kind(i), 0, 0, 0)),
                const((1, d)), const((d, PF_W)), const((d, PA_W))] + [const(t.shape) for t in tables]
    out_specs = [pl.BlockSpec((tm, PF_W), lambda i: (i, 0)),
                 pl.BlockSpec((BATCH, tt, 2 * W_DA), lambda i: (0, i, 0)),
                 pl.BlockSpec((BATCH, tt, W_DA), lambda i: (0, i, 0))]
    out_shape = [jax.ShapeDtypeStruct((n_rows, PF_W), F32),
                 jax.ShapeDtypeStruct((BATCH, s_len, 2 * W_DA), BF16),
                 jax.ShapeDtypeStruct((BATCH, s_len, W_DA), BF16)]
    scratch = pltpu.VMEM((PA_W // LANES, tm, LANES), F32)
    return in_specs, out_specs, out_shape, scratch


def _in_proj(ctx, x, mods, g, wf, wa, tables):
    bsz, n_ctx, d = ctx.shape
    tt = TOKEN_TILE
    assert tt == GRID_W
    tm = tt * BATCH
    ncb = n_ctx // tt
    n = ncb + x.shape[1] // tt
    kind = lambda i: (i >= ncb).astype(jnp.int32)
    in_specs, out_specs, out_shape, scratch = _project_specs(d, tt, n * tm, kind, tables)
    return pl.pallas_call(
        functools.partial(_in_proj_kernel, ncb=ncb),
        grid=(n,),
        in_specs=[pl.BlockSpec((bsz, tt, d), lambda i: (0, jnp.minimum(i, ncb - 1), 0)),
                  pl.BlockSpec((bsz, tt, d), lambda i: (0, jnp.maximum(i - ncb, 0), 0))] + in_specs,
        out_specs=[pl.BlockSpec((tm, d), lambda i: (i, 0))] + out_specs,
        out_shape=[jax.ShapeDtypeStruct((n * tm, d), F32)] + out_shape,
        scratch_shapes=[scratch, pltpu.VMEM((d // LANES, tm, LANES), F32)],
        compiler_params=_cparams(("arbitrary",)),
    )(ctx, x, mods, g.reshape(1, d), wf, wa, *tables)


def _bwd_chunk(i, ncc, n):
    return jnp.where(i < ncc, ncc - 1 - i, n - 1 - i + ncc)


def _lru_kernel(ufc, ufp, ufn, ubc, ubp, ubn, cw_ref, cb_ref, wg_ref, bg_ref, lam_ref,
                hf_ref, hb_ref, a_scr, b_scr, carry_scr, *, ts, ncc, n):
    i = pl.program_id(0)

    @pl.when(i == 0)
    def _():
        carry_scr[...] = jnp.zeros_like(carry_scr)

    def prep(d, c, u_c, u_p, u_n):
        seg_start = jnp.logical_or(c == 0, c == ncc)
        seg_end = jnp.logical_or(c == ncc - 1, c == n - 1)
        prev = u_p[...] * jnp.where(seg_start, 0.0, 1.0)
        nxt = u_n[...] * jnp.where(seg_end, 0.0, 1.0)
        u = jnp.concatenate([prev, u_c[...], nxt], axis=0)
        cw = cw_ref[...]
        xc = cb_ref[...] + cw[0:1] * u[0:ts]
        for j in range(1, CONV_W):
            xc = xc + cw[j:j + 1] * u[j:j + ts]
        xc2 = xc.reshape(ts * xc.shape[1], W_LRU)
        gts = jnp.dot(xc2.astype(BF16), wg_ref[d], preferred_element_type=F32) + bg_ref[d]
        gate_r = jax.nn.sigmoid(gts[:, :W_LRU])
        gate_i = jax.nn.sigmoid(gts[:, W_LRU:])
        log_a = -LRU_C * gate_r * _softplus(-lam_ref[d])
        a = jnp.exp(log_a)
        th = jnp.tanh(log_a)
        one_minus_a2 = -2.0 * th / (1.0 - th)
        bb = jnp.sqrt(one_minus_a2) * (gate_i * xc2)
        a_scr[d] = a.reshape(xc.shape)
        b_scr[d] = bb.reshape(xc.shape)

    prep(0, i, ufc, ufp, ufn)
    prep(1, _bwd_chunk(i, ncc, n), ubc, ubp, ubn)

    def body(k, carry):
        hf, hb = carry
        kb = ts - 1 - k
        hf = a_scr[0, k] * hf + b_scr[0, k]
        hb = a_scr[1, kb] * hb + b_scr[1, kb]
        hf_ref[k] = hf
        hb_ref[kb] = hb
        return hf, hb

    hf, hb = lax.fori_loop(0, ts, body, (carry_scr[0], carry_scr[1]), unroll=8)
    carry_scr[0] = hf
    carry_scr[1] = hb


def _lru(pf3, conv_w, conv_b, wg, bg, lam, *, n_ctx):
    s_len, bsz, _ = pf3.shape
    ts = SCAN_TILE
    n = s_len // ts
    ncc = n_ctx // ts

    def cur(cfn):
        return pl.BlockSpec((ts, bsz, W_LRU), lambda i: (cfn(i), 0, 0))

    def prev(cfn):
        return pl.BlockSpec((2, bsz, W_LRU), lambda i: (jnp.maximum(cfn(i) * (ts // 2) - 1, 0), 0, 0))

    def nxt(cfn):
        return pl.BlockSpec((1, bsz, W_LRU), lambda i: (jnp.minimum((cfn(i) + 1) * ts, s_len - 1), 0, 0))

    fwd = lambda i: i
    bwd = lambda i: _bwd_chunk(i, ncc, n)
    whole = lambda shape: pl.BlockSpec(shape, lambda i: (0,) * len(shape))
    return pl.pallas_call(
        functools.partial(_lru_kernel, ts=ts, ncc=ncc, n=n),
        grid=(n,),
        in_specs=[cur(fwd), prev(fwd), nxt(fwd), cur(bwd), prev(bwd), nxt(bwd),
                  whole(conv_w.shape), whole(conv_b.shape), whole(wg.shape), whole(bg.shape),
                  whole(lam.shape)],
        out_specs=[pl.BlockSpec((ts, bsz, W_LRU), lambda i: (i, 0, 0)),
                   pl.BlockSpec((ts, bsz, W_LRU), lambda i: (bwd(i), 0, 0))],
        out_shape=[jax.ShapeDtypeStruct((s_len, bsz, W_LRU), F32)] * 2,
        scratch_shapes=[pltpu.VMEM((2, ts, bsz, W_LRU), F32),
                        pltpu.VMEM((2, ts, bsz, W_LRU), F32),
                        pltpu.VMEM((2, bsz, W_LRU), F32)],
        compiler_params=_cparams(("arbitrary",)),
    )(pf3, pf3, pf3, pf3, pf3, pf3, conv_w, conv_b, wg, bg, lam)


S5_LANE_CHUNK = 512


def _s5_kernel(uf_ref, ub_ref, lre_ref, lim_ref, ldt_ref, bre_ref, bim_ref, cre_ref, cim_ref,
               yf_ref, yb_ref, bbar_scr, a_scr, buf_scr, carry_scr, *, ts):
    i = pl.program_id(0)
    nst = S5_N
    bsz = uf_ref.shape[1]

    @pl.when(i == 0)
    def _():
        carry_scr[...] = jnp.zeros_like(carry_scr)
        for d in range(2):
            lre = lre_ref[d]
            lim = lim_ref[d]
            dt = jnp.exp(ldt_ref[d])
            mag = jnp.exp(lre * dt)
            ang = lim * dt
            ar = mag * jnp.cos(ang)
            ai = mag * jnp.sin(ang)
            nr = ar - 1.0
            den = lre * lre + lim * lim
            cfr = (nr * lre + ai * lim) / den
            cfi = (ai * lre - nr * lim) / den
            bre = bre_ref[d]
            bim = bim_ref[d]
            bbar_scr[d, :, :nst] = (cfr * bre - cfi * bim).astype(BF16)
            bbar_scr[d, :, nst:] = (cfr * bim + cfi * bre).astype(BF16)
            a_scr[d, 0] = jnp.broadcast_to(ar, (bsz, nst))
            a_scr[d, 1] = jnp.broadcast_to(ai, (bsz, nst))

    for d, u_ref in enumerate((uf_ref, ub_ref)):
        u2 = u_ref[...].reshape(ts * bsz, W_S5).astype(BF16)
        bu = jnp.dot(u2, bbar_scr[d], preferred_element_type=F32)
        buf_scr[d] = bu.reshape(ts, bsz, 2 * nst)

    lc = S5_LANE_CHUNK
    chains = [(d, pl.ds(c0, lc), pl.ds(nst + c0, lc)) for d in range(2) for c0 in range(0, nst, lc)]
    state = [(carry_scr[d, :, re_sl], carry_scr[d, :, im_sl]) for d, re_sl, im_sl in chains]
    for k in range(ts):
        for n_chain, (d, re_sl, im_sl) in enumerate(chains):
            t = k if d == 0 else ts - 1 - k
            ar = a_scr[d, 0, :, re_sl]
            ai = a_scr[d, 1, :, re_sl]
            sre, sim = state[n_chain]
            nre = ar * sre - ai * sim + buf_scr[d, t, :, re_sl]
            nim = ar * sim + ai * sre + buf_scr[d, t, :, im_sl]
            buf_scr[d, t, :, re_sl] = nre
            buf_scr[d, t, :, im_sl] = nim
            state[n_chain] = (nre, nim)
    for (d, re_sl, im_sl), (sre, sim) in zip(chains, state):
        carry_scr[d, :, re_sl] = sre
        carry_scr[d, :, im_sl] = sim

    for d, y_ref in enumerate((yf_ref, yb_ref)):
        st = buf_scr[d].reshape(ts * bsz, 2 * nst)
        y = (jnp.dot(st[:, :nst].astype(BF16), cre_ref[d], preferred_element_type=F32)
             - jnp.dot(st[:, nst:].astype(BF16), cim_ref[d], preferred_element_type=F32))
        y_ref[...] = y.reshape(ts, bsz, W_S5)


def _s5(pf3, lre, lim, ldt, bre, bim, cre, cim, *, n_ctx):
    s_len, bsz, _ = pf3.shape
    ts = SCAN_TILE
    n = s_len // ts
    ncc = n_ctx // ts
    bwd = lambda i: _bwd_chunk(i, ncc, n)
    col = W_LRU // W_S5
    whole = lambda a: pl.BlockSpec(a.shape, lambda i: (0,) * a.ndim)
    return pl.pallas_call(
        functools.partial(_s5_kernel, ts=ts),
        grid=(n,),
        in_specs=[pl.BlockSpec((ts, bsz, W_S5), lambda i: (i, 0, col)),
                  pl.BlockSpec((ts, bsz, W_S5), lambda i: (bwd(i), 0, col)),
                  whole(lre), whole(lim), whole(ldt), whole(bre), whole(bim), whole(cre), whole(cim)],
        out_specs=[pl.BlockSpec((ts, bsz, W_S5), lambda i: (i, 0, 0)),
                   pl.BlockSpec((ts, bsz, W_S5), lambda i: (bwd(i), 0, 0))],
        out_shape=[jax.ShapeDtypeStruct((s_len, bsz, W_S5), F32)] * 2,
        scratch_shapes=[pltpu.VMEM((2, W_S5, 2 * S5_N), BF16),
                        pltpu.VMEM((2, 2, bsz, S5_N), F32),
                        pltpu.VMEM((2, ts, bsz, 2 * S5_N), F32),
                        pltpu.VMEM((2, bsz, 2 * S5_N), F32)],
        compiler_params=_cparams(("arbitrary",)),
    )(pf3, pf3, lre, lim, ldt, bre, bim, cre, cim)


V_ROWS = DA_DV + 16
SCORE_PARTS = 2
SCORE_SLOTS = 4


def _attn_kernel(q_ref, k_ref, v_ref, dl_ref, g_ref, o_ref, s_scr, vt_scr, *, tq, n_ctx, lam_init):
    s_len = k_ref.shape[1]
    nq = s_len // tq
    ncq = n_ctx // tq
    for c0 in range(0, s_len, tq):
        vt_scr[0:DA_DV, c0:c0 + tq] = v_ref[0, c0:c0 + tq, :].astype(F32).T.astype(BF16)
    pad_row = lax.broadcasted_iota(jnp.int32, (V_ROWS - DA_DV, s_len), 0)
    vt_scr[DA_DV:, :] = jnp.where(pad_row == 0, 1.0, 0.0).astype(BF16)
    dl = dl_ref[...]
    lam = (jnp.exp(jnp.sum(dl[0:1] * dl[1:2], axis=-1, keepdims=True))
           - jnp.exp(jnp.sum(dl[2:3] * dl[3:4], axis=-1, keepdims=True)) + lam_init)
    gscale = g_ref[...] * (1.0 - lam_init)
    lane = lax.broadcasted_iota(jnp.int32, (tq, LANES), 1)

    def rows(qt):
        return pl.ds(pl.multiple_of(qt * tq, tq), tq)

    def key_parts(nk, max_parts):
        groups = nk // LANES
        n_parts = min(max_parts, groups)
        bounds = [LANES * ((groups * p) // n_parts) for p in range(n_parts + 1)]
        return tuple(zip(bounds[:-1], bounds[1:]))

    def scores(qt, mp, nk, slot):
        q = q_ref[0, rows(qt), :]
        qm = jnp.where((lane < DA_DH) if mp == 0 else (lane >= DA_DH), q, jnp.zeros_like(q))
        m = None
        for lo, hi in key_parts(nk, SCORE_PARTS):
            s_t = lax.dot_general(k_ref[0, lo:hi, :], qm, (((1,), (1,)), ((), ())),
                                  preferred_element_type=F32)
            s_scr[slot, lo:hi, :] = s_t
            m_part = jnp.max(s_t, axis=0, keepdims=True)
            m = m_part if m is None else jnp.maximum(m, m_part)
        return m

    def values(slot, m, nk):
        p_t = jnp.exp2(s_scr[slot, 0:nk, :] - m).astype(BF16)
        return jnp.dot(vt_scr[:, 0:nk], p_t, preferred_element_type=F32)

    def finish(qt, o1, o2):
        o_t = (o1[:DA_DV] * (1.0 / o1[DA_DV:DA_DV + 1])
               - o2[:DA_DV] * (lam / o2[DA_DV:DA_DV + 1]))
        y_t = o_t * lax.rsqrt(jnp.mean(o_t * o_t, axis=0, keepdims=True) + EPS)
        o_ref[0, rows(qt), :] = y_t.T * gscale

    for qt in range(ncq):
        m0 = scores(qt, 0, n_ctx, 0)
        m1 = scores(qt, 1, n_ctx, 2)
        finish(qt, values(0, m0, n_ctx), values(2, m1, n_ctx))

    def body(j, m0):
        t0 = ncq + 2 * j
        m1 = scores(t0, 1, s_len, 2)
        o1 = values(0, m0, s_len)
        m0b = scores(t0 + 1, 0, s_len, 1)
        o2 = values(2, m1, s_len)
        finish(t0, o1, o2)
        m1b = scores(t0 + 1, 1, s_len, 3)
        o1b = values(1, m0b, s_len)
        m0_next = scores(jnp.minimum(t0 + 2, nq - 1), 0, s_len, 0)
        o2b = values(3, m1b, s_len)
        finish(t0 + 1, o1b, o2b)
        return m0_next

    lax.fori_loop(0, (nq - ncq) // 2, body, scores(ncq, 0, s_len, 0))


def _attention(qk, v, da_lam, da_g, *, n_ctx, lam_init):
    bsz, s_len, _ = qk.shape
    tq = Q_TILE
    assert n_ctx % tq == 0 and (s_len - n_ctx) % (2 * tq) == 0
    return pl.pallas_call(
        functools.partial(_attn_kernel, tq=tq, n_ctx=n_ctx, lam_init=lam_init),
        grid=(bsz, DA_HEADS),
        in_specs=[pl.BlockSpec((1, s_len, LANES), lambda b, h: (b, 0, DA_HEADS + h)),
                  pl.BlockSpec((1, s_len, LANES), lambda b, h: (b, 0, h)),
                  pl.BlockSpec((1, s_len, LANES), lambda b, h: (b, 0, h)),
                  pl.BlockSpec((4, DA_DH), lambda b, h: (0, 0)),
                  pl.BlockSpec((1, DA_DV), lambda b, h: (0, 0))],
        out_specs=pl.BlockSpec((1, s_len, LANES), lambda b, h: (b, 0, h)),
        out_shape=jax.ShapeDtypeStruct((bsz, s_len, W_DA), F32),
        scratch_shapes=[pltpu.VMEM((SCORE_SLOTS, s_len, tq), F32),
                        pltpu.VMEM((V_ROWS, s_len), BF16)],
        compiler_params=_cparams(("arbitrary", "arbitrary")),
    )(qk, qk, v, da_lam.reshape(4, DA_DH), da_g.reshape(1, DA_DV))


def _gelu_tanh(x):
    return 0.5 * x * (1.0 + jnp.tanh(math.sqrt(2.0 / math.pi) * (x + 0.044715 * (x * x * x))))


N_MIX_REFS = 15


def _out_proj_kernel(*refs, final, ncb):
    (hf_ref, hb_ref, yf_ref, yb_ref, us_ref, ga_ref, gs_ref, gd_ref, yd_ref, h_ref, mod_ref,
     d_ref, wglu_ref, bglu_ref, wo_ref) = refs[:N_MIX_REFS]
    yd_scr = refs[-1]
    y_a = (hf_ref[...] + hb_ref[...]) * _silu(ga_ref[...])
    z = _gelu_tanh(d_ref[...] * us_ref[...] + yf_ref[...] + yb_ref[...])
    glu = jax.nn.sigmoid(jnp.dot(z.astype(BF16), wglu_ref[...], preferred_element_type=F32)
                         + bglu_ref[...])
    y_s = z * glu * _silu(gs_ref[...])
    y_d = _batch_to_rows(yd_ref, yd_scr) * _silu(gd_ref[...])
    out = (jnp.dot(y_a.astype(BF16), wo_ref[0:W_LRU, :], preferred_element_type=F32)
           + jnp.dot(y_s.astype(BF16), wo_ref[W_LRU:W_LRU + W_S5, :], preferred_element_type=F32)
           + jnp.dot(y_d.astype(BF16), wo_ref[W_LRU + W_S5:, :], preferred_element_type=F32))
    rows, d = out.shape
    gate = mod_ref[0, 2]
    hn = h_ref[...] + (out.reshape(rows // BATCH, BATCH, d) * gate[None]).reshape(rows, d)
    if final:
        fg_ref, o_ref, out_scr = refs[N_MIX_REFS:-1]
        hn = hn * lax.rsqrt(jnp.mean(hn * hn, axis=-1, keepdims=True) + EPS) * fg_ref[...]
        _rows_to_batch(hn, out_scr, 0, o_ref, 0, F32)
    else:
        n_proj_in = 8
        hs_ref = refs[N_MIX_REFS + n_proj_in]
        hs_ref[...] = hn
        _project(pl.program_id(0), hn, *refs[N_MIX_REFS:N_MIX_REFS + n_proj_in],
                 *refs[N_MIX_REFS + n_proj_in + 1:-1], ncb)


def _out_proj(hf, hb, yf, yb, pf, yd, hs, mods, s5_d, w_glu, b_glu, w_out, *, n_ctx,
              final_g=None, next_proj=None):
    final = final_g is not None
    n_rows, d = hs.shape
    tt = TOKEN_TILE
    tm = tt * BATCH
    ncb = n_ctx // tt
    off = ncb if final else 0
    nblk = n_rows // tm - off
    blk = lambda w, c=0: pl.BlockSpec((tm, w), lambda i: (i + off, c))
    const = lambda a: pl.BlockSpec(a.shape, lambda i: (0,) * a.ndim, pipeline_mode=pl.Buffered(1))
    kind = lambda i: (i + off >= ncb).astype(jnp.int32)
    s5_d = s5_d.reshape(1, W_S5)
    b_glu = b_glu.reshape(1, W_S5)
    in_specs = [blk(W_LRU), blk(W_LRU), blk(W_S5), blk(W_S5),
                blk(W_S5, 1), blk(W_LRU, 2), blk(W_S5, 3), blk(W_DA, 2),
                pl.BlockSpec((BATCH, tt, W_DA), lambda i: (0, i + off, 0)), blk(d),
                pl.BlockSpec((1, 3, BATCH, d), lambda i: (kind(i), 0, 0, 0)),
                const(s5_d), const(w_glu), const(b_glu), const(w_out)]
    operands = [hf, hb, yf, yb, pf, pf, pf, pf, yd, hs, mods, s5_d, w_glu, b_glu, w_out]
    assert len(operands) == N_MIX_REFS
    yd_scratch = pltpu.VMEM((W_DA // LANES, tm, LANES), F32)
    if final:
        final_g = final_g.reshape(1, d)
        in_specs += [const(final_g)]
        operands += [final_g]
        out_specs = pl.BlockSpec((BATCH, tt, d), lambda i: (0, i, 0))
        out_shape = jax.ShapeDtypeStruct((BATCH, nblk * tt, d), F32)
        scratch = [pltpu.VMEM((d // LANES, tm, LANES), F32), yd_scratch]
    else:
        mods2, g2, wf, wa, tables = next_proj
        p_in, p_out, p_shape, p_scratch = _project_specs(d, tt, n_rows, kind, tables)
        in_specs += p_in
        operands += [mods2, g2.reshape(1, d), wf, wa, *tables]
        out_specs = [pl.BlockSpec((tm, d), lambda i: (i, 0))] + p_out
        out_shape = [jax.ShapeDtypeStruct((n_rows, d), F32)] + p_shape
        scratch = [p_scratch, yd_scratch]
    return pl.pallas_call(
        functools.partial(_out_proj_kernel, final=final, ncb=ncb),
        grid=(nblk,),
        in_specs=in_specs,
        out_specs=out_specs,
        out_shape=out_shape,
        scratch_shapes=scratch,
        compiler_params=_cparams(("arbitrary",)),
    )(*operands)


def _block_diag(w):
    nb, k, n = w.shape
    eye = jnp.eye(nb, dtype=w.dtype)
    return jnp.einsum('nkd,nm->nkmd', w, eye).reshape(nb * k, nb * n)


def kernel(x, c, ctx, c_ctx, norm_g, w_mod, b_mod, w_in, w_out, lru_conv_w, lru_conv_b, lru_wa, lru_ba,
           lru_wx, lru_bx, lru_lam, s5_lam_re, s5_lam_im, s5_log_dt, s5_b_re, s5_b_im, s5_c_re, s5_c_im,
           s5_d, s5_w_glu, s5_b_glu, da_lam, da_norm_g, final_g):
    bsz, t_len, d = x.shape
    n_ctx = ctx.shape[1]
    s_len = n_ctx + t_len
    n_rows = s_len * bsz
    n_layers = w_in.shape[0]
    assert bsz == BATCH and bsz + 1 <= MOD_ROWS
    assert n_ctx % Q_TILE == 0 and t_len % Q_TILE == 0
    assert n_ctx % TOKEN_TILE == 0 and n_ctx % SCAN_TILE == 0


    cc = jnp.concatenate([c, c_ctx[None], jnp.zeros((MOD_ROWS - bsz - 1, d), F32)], axis=0)
    mod = _modulation(cc, w_mod.astype(BF16), b_mod).reshape(n_layers, MOD_ROWS, 3, d)
    mod_lat = mod[:, :bsz].transpose(0, 2, 1, 3)
    mod_ctx = jnp.broadcast_to(mod[:, bsz:bsz + 1], (n_layers, bsz, 3, d)).transpose(0, 2, 1, 3)
    mods = jnp.stack([mod_ctx, mod_lat], axis=1)

    tables = _rope_tables()

    side = W_LRU + W_S5 + 2 * W_DA
    c_ua, c_us = (0, W_LRU), (W_LRU, W_LRU + W_S5)
    c_k, c_v = (W_LRU + W_S5, W_LRU + W_S5 + W_DA), (W_LRU + W_S5 + W_DA, side)
    sl = lambda lo_hi, base=0: w_in[:, :, base + lo_hi[0]:base + lo_hi[1]]
    wf = jnp.concatenate([sl(c_ua), sl(c_us), sl(c_ua, side), sl(c_us, side), sl(c_v, side)],
                         axis=-1).astype(BF16)
    wa = jnp.concatenate([sl(c_k), sl(c_v), sl(c_k, side)], axis=-1).astype(BF16)
    w_out_b = w_out.astype(BF16)
    w_glu_b = s5_w_glu.astype(BF16)

    hs, pf, kq, v = _in_proj(ctx, x, mods[0], norm_g[0], wf[0], wa[0], tables)
    for l in range(n_layers):
        final = l == n_layers - 1
        lam_init = 0.8 - 0.6 * math.exp(-0.3 * l)
        pf3 = pf.reshape(s_len, bsz, PF_W)

        wg = jnp.stack([jnp.concatenate([_block_diag(lru_wa[l, dd]), _block_diag(lru_wx[l, dd])], axis=1)
                        for dd in range(2)]).astype(BF16)
        bg = jnp.concatenate([lru_ba[l], lru_bx[l]], axis=-1).reshape(2, 1, 2 * W_LRU)
        hf, hb = _lru(pf3, lru_conv_w[l], lru_conv_b[l].reshape(1, W_LRU), wg, bg,
                      lru_lam[l].reshape(2, 1, W_LRU), n_ctx=n_ctx)

        flat = lambda a: a.reshape(2, 1, S5_N)
        ldt = jnp.broadcast_to(s5_log_dt[l][:, :, None], (2, S5_G, S5_P))
        bre = jnp.stack([_block_diag(jnp.swapaxes(s5_b_re[l, dd], 1, 2)) for dd in range(2)])
        bim = jnp.stack([_block_diag(jnp.swapaxes(s5_b_im[l, dd], 1, 2)) for dd in range(2)])
        cre = jnp.stack([_block_diag(jnp.swapaxes(s5_c_re[l, dd], 1, 2)) for dd in range(2)]).astype(BF16)
        cim = jnp.stack([_block_diag(jnp.swapaxes(s5_c_im[l, dd], 1, 2)) for dd in range(2)]).astype(BF16)
        yf, yb = _s5(pf3, flat(s5_lam_re[l]), flat(s5_lam_im[l]), flat(ldt), bre, bim, cre, cim,
                     n_ctx=n_ctx)

        yd = _attention(kq, v, da_lam[l], da_norm_g[l], n_ctx=n_ctx, lam_init=lam_init)

        mix = (hf.reshape(n_rows, W_LRU), hb.reshape(n_rows, W_LRU),
               yf.reshape(n_rows, W_S5), yb.reshape(n_rows, W_S5),
               pf, yd, hs, mods[l], s5_d[l], w_glu_b[l], s5_b_glu[l], w_out_b[l])
        if final:
            return _out_proj(*mix, n_ctx=n_ctx, final_g=final_g)
        hs, pf, kq, v = _out_proj(*mix, n_ctx=n_ctx, next_proj=(
            mods[l + 1], norm_g[l + 1], wf[l + 1], wa[l + 1], tables))
```

```python
import functools
import math

import jax
import jax.numpy as jnp
from jax import lax
from jax.experimental import pallas as pl
from jax.experimental.pallas import tpu as pltpu

F32 = jnp.float32
BF16 = jnp.bfloat16

GRID_W = 64
EPS = 1e-6
W_LRU = 256
LRU_BLOCKS = 4
CONV_W = 4
LRU_C = 8.0
W_S5 = 256
S5_H = 16
S5_G = 16
S5_P = 64
S5_N = S5_G * S5_P
DA_HEADS = 4
DA_DH = 64
DA_DV = 2 * DA_DH
W_DA = DA_HEADS * DA_DV
ROPE_BASE = 10000.0
ROPE_F = DA_DH // 4
LANES = 128
BATCH = 8

PF_W = 2 * W_LRU + 2 * W_S5 + W_DA
PA_W = 3 * W_DA

TOKEN_TILE = 64
SCAN_TILE = 64
Q_TILE = 256
MOD_ROWS = 16
VMEM_LIMIT = 48 * 1024 * 1024


def _cparams(sem):
    return pltpu.CompilerParams(dimension_semantics=sem, vmem_limit_bytes=VMEM_LIMIT)


def _silu(x):
    return x * jax.nn.sigmoid(x)


def _softplus(x):
    return jnp.maximum(x, 0.0) + jnp.log1p(jnp.exp(-jnp.abs(x)))


def _rows_to_batch(val, scr, slab0, out_ref, col0, dtype):
    rows, width = val.shape
    tt = rows // BATCH
    for j in range(width // LANES):
        scr[slab0 + j] = val[:, j * LANES:(j + 1) * LANES]
    for b in range(BATCH):
        for j in range(width // LANES):
            out_ref[b, :, col0 + j * LANES:col0 + (j + 1) * LANES] = (
                scr[slab0 + j, pl.ds(b, tt, stride=BATCH), :].astype(dtype))


def _batch_to_rows(in_ref, scr):
    _, tt, width = in_ref.shape
    for b in range(BATCH):
        for j in range(width // LANES):
            scr[j, pl.ds(b, tt, stride=BATCH), :] = in_ref[b, :, j * LANES:(j + 1) * LANES].astype(F32)
    return jnp.concatenate([scr[j] for j in range(width // LANES)], axis=1)


def _rope_table_kernel(csp_ref, snp_ref, csr_ref, snr_ref):
    def table(shape, pos_shift):
        pos = lax.broadcasted_iota(jnp.int32, shape, 0) >> pos_shift
        lane = lax.broadcasted_iota(jnp.int32, shape, 1)
        f = (lane & (ROPE_F - 1)).astype(F32)
        ang = pos.astype(F32) * jnp.exp(-math.log(ROPE_BASE) * f / ROPE_F)
        sn = jnp.sin(ang)
        return jnp.cos(ang), jnp.where((lane & ROPE_F) == 0, -sn, sn)

    csp_ref[...], snp_ref[...] = table(csp_ref.shape, 0)
    csr_ref[...], snr_ref[...] = table(csr_ref.shape, int(math.log2(BATCH)))


def _rope_tables():
    shapes = [(GRID_W, LANES)] * 2 + [(GRID_W * BATCH, LANES)] * 2
    return pl.pallas_call(
        _rope_table_kernel,
        out_shape=[jax.ShapeDtypeStruct(s, F32) for s in shapes],
        compiler_params=pltpu.CompilerParams(vmem_limit_bytes=VMEM_LIMIT),
    )()


def _mod_kernel(c_ref, w_ref, b_ref, o_ref):
    c = c_ref[...]
    o_ref[0] = jnp.dot(_silu(c).astype(BF16), w_ref[0], preferred_element_type=F32) + b_ref[0]


def _modulation(cc, w_mod, b_mod):
    n_layers, d, d3 = w_mod.shape
    return pl.pallas_call(
        _mod_kernel,
        grid=(n_layers, d3 // d),
        in_specs=[pl.BlockSpec((MOD_ROWS, d), lambda l, j: (0, 0)),
                  pl.BlockSpec((1, d, d), lambda l, j: (l, 0, j)),
                  pl.BlockSpec((1, 1, d), lambda l, j: (l, 0, j))],
        out_specs=pl.BlockSpec((1, MOD_ROWS, d), lambda l, j: (l, 0, j)),
        out_shape=jax.ShapeDtypeStruct((n_layers, MOD_ROWS, d3), F32),
        compiler_params=_cparams(("arbitrary", "arbitrary")),
    )(cc, w_mod, b_mod.reshape(n_layers, 1, d3))


def _rope(x, cos, sin):
    width = x.shape[1]
    reps = width // LANES
    cos = jnp.concatenate([cos] * reps, axis=1)
    sin = jnp.concatenate([sin] * reps, axis=1)
    lane = lax.broadcasted_iota(jnp.int32, x.shape, 1)
    first = (lane & ROPE_F) == 0
    partner = jnp.where(first, pltpu.roll(x, width - ROPE_F, 1), pltpu.roll(x, ROPE_F, 1))
    return x * cos + partner * sin


def _project(i, x, mod_ref, g_ref, wf_ref, wa_ref, csp_ref, snp_ref, csr_ref, snr_ref,
             pf_ref, kq_ref, v_ref, scr, ncb):
    rows, d = x.shape
    y = x * lax.rsqrt(jnp.mean(x * x, axis=-1, keepdims=True) + EPS) * g_ref[...]
    shift = mod_ref[0, 0]
    scale = mod_ref[0, 1]
    y3 = y.reshape(rows // BATCH, BATCH, d)
    n = (y3 * (1.0 + scale)[None] + shift[None]).reshape(rows, d).astype(BF16)
    pf_ref[...] = jnp.dot(n, wf_ref[...], preferred_element_type=F32)
    att = jnp.dot(n, wa_ref[...], preferred_element_type=F32)
    grid_row = jnp.maximum(i - ncb, 0)
    is_ctx = i < ncb
    lane = lax.broadcasted_iota(jnp.int32, (rows, LANES), 1)
    use_col = (lane & (DA_DH // 2)) != 0
    cos = jnp.where(use_col, csr_ref[...], csp_ref[pl.ds(grid_row, 1), :])
    sin = jnp.where(use_col, snr_ref[...], snp_ref[pl.ds(grid_row, 1), :])
    cos = jnp.where(is_ctx, 1.0, cos)
    sin = jnp.where(is_ctx, 0.0, sin)

    k = _rope(att[:, :W_DA], cos, sin)
    v = att[:, W_DA:2 * W_DA]
    q = _rope(att[:, 2 * W_DA:], cos, sin) * (DA_DH ** -0.5 * math.log2(math.e))
    slabs = W_DA // LANES
    _rows_to_batch(k, scr, 0, kq_ref, 0, BF16)
    _rows_to_batch(q, scr, slabs, kq_ref, W_DA, BF16)
    _rows_to_batch(v, scr, 2 * slabs, v_ref, 0, BF16)


def _in_proj_kernel(ctx_ref, x_ref, *refs, ncb):
    proj_in, (hs_ref, pf_ref, kq_ref, v_ref, scr, rows_scr) = refs[:8], refs[8:]
    i = pl.program_id(0)

    @pl.when(i < ncb)
    def _():
        hs_ref[...] = _batch_to_rows(ctx_ref, rows_scr)

    @pl.when(i >= ncb)
    def _():
        hs_ref[...] = _batch_to_rows(x_ref, rows_scr)

    _project(i, hs_ref[...], *proj_in, pf_ref, kq_ref, v_ref, scr, ncb)


def _project_specs(d, tt, n_rows, kind, tables):
    tm = tt * BATCH
    s_len = n_rows // BATCH
    const = lambda shape: pl.BlockSpec(shape, lambda i: (0,) * len(shape),
                                       pipeline_mode=pl.Buffered(1))
    in_specs = [pl.BlockSpec((1, 3, BATCH, d), lambda i: (kind(i), 0, 0, 0)),
                const((1, d)), const((d, PF_W)), const((d, PA_W))] + [const(t.shape) for t in tables]
    out_specs = [pl.BlockSpec((tm, PF_W), lambda i: (i, 0)),
                 pl.BlockSpec((BATCH, tt, 2 * W_DA), lambda i: (0, i, 0)),
                 pl.BlockSpec((BATCH, tt, W_DA), lambda i: (0, i, 0))]
    out_shape = [jax.ShapeDtypeStruct((n_rows, PF_W), F32),
                 jax.ShapeDtypeStruct((BATCH, s_len, 2 * W_DA), BF16),
                 jax.ShapeDtypeStruct((BATCH, s_len, W_DA), BF16)]
    scratch = pltpu.VMEM((PA_W // LANES, tm, LANES), F32)
    return in_specs, out_specs, out_shape, scratch


def _in_proj(ctx, x, mods, g, wf, wa, tables):
    bsz, n_ctx, d = ctx.shape
    tt = TOKEN_TILE
    assert tt == GRID_W
    tm = tt * BATCH
    ncb = n_ctx // tt
    n = ncb + x.shape[1] // tt
    kind = lambda i: (i >= ncb).astype(jnp.int32)
    in_specs, out_specs, out_shape, scratch = _project_specs(d, tt, n * tm, kind, tables)
    return pl.pallas_call(
        functools.partial(_in_proj_kernel, ncb=ncb),
        grid=(n,),
        in_specs=[pl.BlockSpec((bsz, tt, d), lambda i: (0, jnp.minimum(i, ncb - 1), 0)),
                  pl.BlockSpec((bsz, tt, d), lambda i: (0, jnp.maximum(i - ncb, 0), 0))] + in_specs,
        out_specs=[pl.BlockSpec((tm, d), lambda i: (i, 0))] + out_specs,
        out_shape=[jax.ShapeDtypeStruct((n * tm, d), F32)] + out_shape,
        scratch_shapes=[scratch, pltpu.VMEM((d // LANES, tm, LANES), F32)],
        compiler_params=_cparams(("arbitrary",)),
    )(ctx, x, mods, g.reshape(1, d), wf, wa, *tables)


def _bwd_chunk(i, ncc, n):
    return jnp.where(i < ncc, ncc - 1 - i, n - 1 - i + ncc)


S5_LANE_CHUNK = 512
N_LRU_IN, N_S5_IN = 11, 9


def _lru_gates(d, c, u_c, u_p, u_n, cw_ref, cb_ref, wg_ref, bg_ref, lam_ref, a_scr, b_scr, ts, ncc, n):
    seg_start = jnp.logical_or(c == 0, c == ncc)
    seg_end = jnp.logical_or(c == ncc - 1, c == n - 1)
    prev = u_p[...] * jnp.where(seg_start, 0.0, 1.0)
    nxt = u_n[...] * jnp.where(seg_end, 0.0, 1.0)
    u = jnp.concatenate([prev, u_c[...], nxt], axis=0)
    cw = cw_ref[...]
    xc = cb_ref[...] + cw[0:1] * u[0:ts]
    for j in range(1, CONV_W):
        xc = xc + cw[j:j + 1] * u[j:j + ts]
    xc2 = xc.reshape(ts * xc.shape[1], W_LRU)
    gts = jnp.dot(xc2.astype(BF16), wg_ref[d], preferred_element_type=F32) + bg_ref[d]
    gate_r = jax.nn.sigmoid(gts[:, :W_LRU])
    gate_i = jax.nn.sigmoid(gts[:, W_LRU:])
    log_a = -LRU_C * gate_r * _softplus(-lam_ref[d])
    a = jnp.exp(log_a)
    th = jnp.tanh(log_a)
    one_minus_a2 = -2.0 * th / (1.0 - th)
    bb = jnp.sqrt(one_minus_a2) * (gate_i * xc2)
    a_scr[d] = a.reshape(xc.shape)
    b_scr[d] = bb.reshape(xc.shape)


def _s5_discretise(lre_ref, lim_ref, ldt_ref, bre_ref, bim_ref, bbar_scr, a_scr):
    bsz, nst = a_scr.shape[2:]
    for d in range(2):
        lre = lre_ref[d]
        lim = lim_ref[d]
        dt = jnp.exp(ldt_ref[d])
        mag = jnp.exp(lre * dt)
        ang = lim * dt
        ar = mag * jnp.cos(ang)
        ai = mag * jnp.sin(ang)
        nr = ar - 1.0
        den = lre * lre + lim * lim
        cfr = (nr * lre + ai * lim) / den
        cfi = (ai * lre - nr * lim) / den
        bre = bre_ref[d]
        bim = bim_ref[d]
        bbar_scr[d, :, :nst] = (cfr * bre - cfi * bim).astype(BF16)
        bbar_scr[d, :, nst:] = (cfr * bim + cfi * bre).astype(BF16)
        a_scr[d, 0] = jnp.broadcast_to(ar, (bsz, nst))
        a_scr[d, 1] = jnp.broadcast_to(ai, (bsz, nst))


def _scan_kernel(*refs, ts, ncc, n):
    lru_in, s5_in = refs[:N_LRU_IN], refs[N_LRU_IN:N_LRU_IN + N_S5_IN]
    hf_ref, hb_ref, yf_ref, yb_ref = refs[N_LRU_IN + N_S5_IN:N_LRU_IN + N_S5_IN + 4]
    la_scr, lb_scr, lcarry_scr, bbar_scr, sa_scr, buf_scr, scarry_scr = refs[N_LRU_IN + N_S5_IN + 4:]
    ufc, ufp, ufn, ubc, ubp, ubn, cw_ref, cb_ref, wg_ref, bg_ref, lam_ref = lru_in
    uf_ref, ub_ref, lre_ref, lim_ref, ldt_ref, bre_ref, bim_ref, cre_ref, cim_ref = s5_in
    i = pl.program_id(0)
    nst = S5_N
    bsz = uf_ref.shape[1]

    @pl.when(i == 0)
    def _():
        lcarry_scr[...] = jnp.zeros_like(lcarry_scr)
        scarry_scr[...] = jnp.zeros_like(scarry_scr)
        _s5_discretise(lre_ref, lim_ref, ldt_ref, bre_ref, bim_ref, bbar_scr, sa_scr)

    for d, u_ref in enumerate((uf_ref, ub_ref)):
        u2 = u_ref[...].reshape(ts * bsz, W_S5).astype(BF16)
        bu = jnp.dot(u2, bbar_scr[d], preferred_element_type=F32)
        buf_scr[d] = bu.reshape(ts, bsz, 2 * nst)

    lru_par = (cw_ref, cb_ref, wg_ref, bg_ref, lam_ref, la_scr, lb_scr, ts, ncc, n)
    _lru_gates(0, i, ufc, ufp, ufn, *lru_par)
    _lru_gates(1, _bwd_chunk(i, ncc, n), ubc, ubp, ubn, *lru_par)
    hf = lcarry_scr[0]
    hb = lcarry_scr[1]
    for k in range(ts):
        kb = ts - 1 - k
        hf = la_scr[0, k] * hf + lb_scr[0, k]
        hb = la_scr[1, kb] * hb + lb_scr[1, kb]
        hf_ref[k] = hf
        hb_ref[kb] = hb
    lcarry_scr[0] = hf
    lcarry_scr[1] = hb

    lc = S5_LANE_CHUNK
    chains = [(d, pl.ds(c0, lc), pl.ds(nst + c0, lc)) for d in range(2) for c0 in range(0, nst, lc)]
    state = [(scarry_scr[d, :, re_sl], scarry_scr[d, :, im_sl]) for d, re_sl, im_sl in chains]
    for k in range(ts):
        for n_chain, (d, re_sl, im_sl) in enumerate(chains):
            t = k if d == 0 else ts - 1 - k
            ar = sa_scr[d, 0, :, re_sl]
            ai = sa_scr[d, 1, :, re_sl]
            sre, sim = state[n_chain]
            nre = ar * sre - ai * sim + buf_scr[d, t, :, re_sl]
            nim = ar * sim + ai * sre + buf_scr[d, t, :, im_sl]
            buf_scr[d, t, :, re_sl] = nre
            buf_scr[d, t, :, im_sl] = nim
            state[n_chain] = (nre, nim)
    for (d, re_sl, im_sl), (sre, sim) in zip(chains, state):
        scarry_scr[d, :, re_sl] = sre
        scarry_scr[d, :, im_sl] = sim

    for d, y_ref in enumerate((yf_ref, yb_ref)):
        st = buf_scr[d].reshape(ts * bsz, 2 * nst)
        y = (jnp.dot(st[:, :nst].astype(BF16), cre_ref[d], preferred_element_type=F32)
             - jnp.dot(st[:, nst:].astype(BF16), cim_ref[d], preferred_element_type=F32))
        y_ref[...] = y.reshape(ts, bsz, W_S5)


def _scans(pf3, lru_params, s5_params, *, n_ctx):
    s_len, bsz, _ = pf3.shape
    ts = SCAN_TILE
    n = s_len // ts
    ncc = n_ctx // ts
    fwd = lambda i: i
    bwd = lambda i: _bwd_chunk(i, ncc, n)
    s5_col = W_LRU // W_S5

    def cur(cfn, col=0):
        return pl.BlockSpec((ts, bsz, W_LRU), lambda i: (cfn(i), 0, col))

    def prev(cfn):
        return pl.BlockSpec((2, bsz, W_LRU), lambda i: (jnp.maximum(cfn(i) * (ts // 2) - 1, 0), 0, 0))

    def nxt(cfn):
        return pl.BlockSpec((1, bsz, W_LRU), lambda i: (jnp.minimum((cfn(i) + 1) * ts, s_len - 1), 0, 0))

    const = lambda a: pl.BlockSpec(a.shape, lambda i: (0,) * a.ndim, pipeline_mode=pl.Buffered(1))
    assert len(lru_params) == N_LRU_IN - 6 and len(s5_params) == N_S5_IN - 2
    out_block = lambda cfn: pl.BlockSpec((ts, bsz, W_LRU), lambda i: (cfn(i), 0, 0))
    return pl.pallas_call(
        functools.partial(_scan_kernel, ts=ts, ncc=ncc, n=n),
        grid=(n,),
        in_specs=([cur(fwd), prev(fwd), nxt(fwd), cur(bwd), prev(bwd), nxt(bwd)]
                  + [const(a) for a in lru_params]
                  + [cur(fwd, s5_col), cur(bwd, s5_col)] + [const(a) for a in s5_params]),
        out_specs=[out_block(fwd), out_block(bwd), out_block(fwd), out_block(bwd)],
        out_shape=[jax.ShapeDtypeStruct((s_len, bsz, W_LRU), F32)] * 4,
        scratch_shapes=[pltpu.VMEM((2, ts, bsz, W_LRU), F32),
                        pltpu.VMEM((2, ts, bsz, W_LRU), F32),
                        pltpu.VMEM((2, bsz, W_LRU), F32),
                        pltpu.VMEM((2, W_S5, 2 * S5_N), BF16),
                        pltpu.VMEM((2, 2, bsz, S5_N), F32),
                        pltpu.VMEM((2, ts, bsz, 2 * S5_N), F32),
                        pltpu.VMEM((2, bsz, 2 * S5_N), F32)],
        compiler_params=_cparams(("arbitrary",)),
    )(*([pf3] * 6), *lru_params, pf3, pf3, *s5_params)


V_ROWS = DA_DV + 16
SCORE_PARTS = 2
SCORE_SLOTS = 4


def _attn_kernel(q_ref, k_ref, v_ref, dl_ref, g_ref, o_ref, s_scr, vt_scr, *, tq, n_ctx, lam_init):
    s_len = k_ref.shape[1]
    nq = s_len // tq
    ncq = n_ctx // tq
    for c0 in range(0, s_len, tq):
        vt_scr[0:DA_DV, c0:c0 + tq] = v_ref[0, c0:c0 + tq, :].astype(F32).T.astype(BF16)
    pad_row = lax.broadcasted_iota(jnp.int32, (V_ROWS - DA_DV, s_len), 0)
    vt_scr[DA_DV:, :] = jnp.where(pad_row == 0, 1.0, 0.0).astype(BF16)
    dl = dl_ref[...]
    lam = (jnp.exp(jnp.sum(dl[0:1] * dl[1:2], axis=-1, keepdims=True))
           - jnp.exp(jnp.sum(dl[2:3] * dl[3:4], axis=-1, keepdims=True)) + lam_init)
    gscale = g_ref[...] * (1.0 - lam_init)
    lane = lax.broadcasted_iota(jnp.int32, (tq, LANES), 1)

    def rows(qt):
        return pl.ds(pl.multiple_of(qt * tq, tq), tq)

    def key_parts(nk, max_parts):
        groups = nk // LANES
        n_parts = min(max_parts, groups)
        bounds = [LANES * ((groups * p) // n_parts) for p in range(n_parts + 1)]
        return tuple(zip(bounds[:-1], bounds[1:]))

    def scores(qt, mp, nk, slot):
        q = q_ref[0, rows(qt), :]
        qm = jnp.where((lane < DA_DH) if mp == 0 else (lane >= DA_DH), q, jnp.zeros_like(q))
        m = None
        for lo, hi in key_parts(nk, SCORE_PARTS):
            s_t = lax.dot_general(k_ref[0, lo:hi, :], qm, (((1,), (1,)), ((), ())),
                                  preferred_element_type=F32)
            s_scr[slot, lo:hi, :] = s_t
            m_part = jnp.max(s_t, axis=0, keepdims=True)
            m = m_part if m is None else jnp.maximum(m, m_part)
        return m

    def values(slot, m, nk):
        p_t = jnp.exp2(s_scr[slot, 0:nk, :] - m).astype(BF16)
        return jnp.dot(vt_scr[:, 0:nk], p_t, preferred_element_type=F32)

    def finish(qt, o1, o2):
        o_t = (o1[:DA_DV] * (1.0 / o1[DA_DV:DA_DV + 1])
               - o2[:DA_DV] * (lam / o2[DA_DV:DA_DV + 1]))
        y_t = o_t * lax.rsqrt(jnp.mean(o_t * o_t, axis=0, keepdims=True) + EPS)
        o_ref[0, rows(qt), :] = y_t.T * gscale

    for qt in range(ncq):
        m0 = scores(qt, 0, n_ctx, 0)
        m1 = scores(qt, 1, n_ctx, 2)
        finish(qt, values(0, m0, n_ctx), values(2, m1, n_ctx))

    def body(j, m0):
        t0 = ncq + 2 * j
        m1 = scores(t0, 1, s_len, 2)
        o1 = values(0, m0, s_len)
        m0b = scores(t0 + 1, 0, s_len, 1)
        o2 = values(2, m1, s_len)
        finish(t0, o1, o2)
        m1b = scores(t0 + 1, 1, s_len, 3)
        o1b = values(1, m0b, s_len)
        m0_next = scores(jnp.minimum(t0 + 2, nq - 1), 0, s_len, 0)
        o2b = values(3, m1b, s_len)
        finish(t0 + 1, o1b, o2b)
        return m0_next

    lax.fori_loop(0, (nq - ncq) // 2, body, scores(ncq, 0, s_len, 0))


def _attention(qk, v, da_lam, da_g, *, n_ctx, lam_init):
    bsz, s_len, _ = qk.shape
    tq = Q_TILE
    assert n_ctx % tq == 0 and (s_len - n_ctx) % (2 * tq) == 0
    return pl.pallas_call(
        functools.partial(_attn_kernel, tq=tq, n_ctx=n_ctx, lam_init=lam_init),
        grid=(bsz, DA_HEADS),
        in_specs=[pl.BlockSpec((1, s_len, LANES), lambda b, h: (b, 0, DA_HEADS + h)),
                  pl.BlockSpec((1, s_len, LANES), lambda b, h: (b, 0, h)),
                  pl.BlockSpec((1, s_len, LANES), lambda b, h: (b, 0, h)),
                  pl.BlockSpec((4, DA_DH), lambda b, h: (0, 0)),
                  pl.BlockSpec((1, DA_DV), lambda b, h: (0, 0))],
        out_specs=pl.BlockSpec((1, s_len, LANES), lambda b, h: (b, 0, h)),
        out_shape=jax.ShapeDtypeStruct((bsz, s_len, W_DA), F32),
        scratch_shapes=[pltpu.VMEM((SCORE_SLOTS, s_len, tq), F32),
                        pltpu.VMEM((V_ROWS, s_len), BF16)],
        compiler_params=_cparams(("arbitrary", "arbitrary")),
    )(qk, qk, v, da_lam.reshape(4, DA_DH), da_g.reshape(1, DA_DV))


def _gelu_tanh(x):
    return 0.5 * x * (1.0 + jnp.tanh(math.sqrt(2.0 / math.pi) * (x + 0.044715 * (x * x * x))))


N_MIX_REFS = 15


def _out_proj_kernel(*refs, final, ncb):
    (hf_ref, hb_ref, yf_ref, yb_ref, us_ref, ga_ref, gs_ref, gd_ref, yd_ref, h_ref, mod_ref,
     d_ref, wglu_ref, bglu_ref, wo_ref) = refs[:N_MIX_REFS]
    yd_scr = refs[-1]
    y_a = (hf_ref[...] + hb_ref[...]) * _silu(ga_ref[...])
    z = _gelu_tanh(d_ref[...] * us_ref[...] + yf_ref[...] + yb_ref[...])
    glu = jax.nn.sigmoid(jnp.dot(z.astype(BF16), wglu_ref[...], preferred_element_type=F32)
                         + bglu_ref[...])
    y_s = z * glu * _silu(gs_ref[...])
    y_d = _batch_to_rows(yd_ref, yd_scr) * _silu(gd_ref[...])
    out = (jnp.dot(y_a.astype(BF16), wo_ref[0:W_LRU, :], preferred_element_type=F32)
           + jnp.dot(y_s.astype(BF16), wo_ref[W_LRU:W_LRU + W_S5, :], preferred_element_type=F32)
           + jnp.dot(y_d.astype(BF16), wo_ref[W_LRU + W_S5:, :], preferred_element_type=F32))
    rows, d = out.shape
    gate = mod_ref[0, 2]
    hn = h_ref[...] + (out.reshape(rows // BATCH, BATCH, d) * gate[None]).reshape(rows, d)
    if final:
        fg_ref, o_ref, out_scr = refs[N_MIX_REFS:-1]
        hn = hn * lax.rsqrt(jnp.mean(hn * hn, axis=-1, keepdims=True) + EPS) * fg_ref[...]
        _rows_to_batch(hn, out_scr, 0, o_ref, 0, F32)
    else:
        n_proj_in = 8
        hs_ref = refs[N_MIX_REFS + n_proj_in]
        hs_ref[...] = hn
        _project(pl.program_id(0), hn, *refs[N_MIX_REFS:N_MIX_REFS + n_proj_in],
                 *refs[N_MIX_REFS + n_proj_in + 1:-1], ncb)


def _out_proj(hf, hb, yf, yb, pf, yd, hs, mods, s5_d, w_glu, b_glu, w_out, *, n_ctx,
              final_g=None, next_proj=None):
    final = final_g is not None
    n_rows, d = hs.shape
    tt = TOKEN_TILE
    tm = tt * BATCH
    ncb = n_ctx // tt
    off = ncb if final else 0
    nblk = n_rows // tm - off
    blk = lambda w, c=0: pl.BlockSpec((tm, w), lambda i: (i + off, c))
    const = lambda a: pl.BlockSpec(a.shape, lambda i: (0,) * a.ndim, pipeline_mode=pl.Buffered(1))
    kind = lambda i: (i + off >= ncb).astype(jnp.int32)
    s5_d = s5_d.reshape(1, W_S5)
    b_glu = b_glu.reshape(1, W_S5)
    in_specs = [blk(W_LRU), blk(W_LRU), blk(W_S5), blk(W_S5),
                blk(W_S5, 1), blk(W_LRU, 2), blk(W_S5, 3), blk(W_DA, 2),
                pl.BlockSpec((BATCH, tt, W_DA), lambda i: (0, i + off, 0)), blk(d),
                pl.BlockSpec((1, 3, BATCH, d), lambda i: (kind(i), 0, 0, 0)),
                const(s5_d), const(w_glu), const(b_glu), const(w_out)]
    operands = [hf, hb, yf, yb, pf, pf, pf, pf, yd, hs, mods, s5_d, w_glu, b_glu, w_out]
    assert len(operands) == N_MIX_REFS
    yd_scratch = pltpu.VMEM((W_DA // LANES, tm, LANES), F32)
    if final:
        final_g = final_g.reshape(1, d)
        in_specs += [const(final_g)]
        operands += [final_g]
        out_specs = pl.BlockSpec((BATCH, tt, d), lambda i: (0, i, 0))
        out_shape = jax.ShapeDtypeStruct((BATCH, nblk * tt, d), F32)
        scratch = [pltpu.VMEM((d // LANES, tm, LANES), F32), yd_scratch]
    else:
        mods2, g2, wf, wa, tables = next_proj
        p_in, p_out, p_shape, p_scratch = _project_specs(d, tt, n_rows, kind, tables)
        in_specs += p_in
        operands += [mods2, g2.reshape(1, d), wf, wa, *tables]
        out_specs = [pl.BlockSpec((tm, d), lambda i: (i, 0))] + p_out
        out_shape = [jax.ShapeDtypeStruct((n_rows, d), F32)] + p_shape
        scratch = [p_scratch, yd_scratch]
    return pl.pallas_call(
        functools.partial(_out_proj_kernel, final=final, ncb=ncb),
        grid=(nblk,),
        in_specs=in_specs,
        out_specs=out_specs,
        out_shape=out_shape,
        scratch_shapes=scratch,
        compiler_params=_cparams(("arbitrary",)),
    )(*operands)


def _block_diag(w):
    m, nb, k, n = w.shape
    eye = jnp.eye(nb, dtype=w.dtype)
    return jnp.einsum('bnkd,nm->bnkmd', w, eye).reshape(m, nb * k, nb * n)


def kernel(x, c, ctx, c_ctx, norm_g, w_mod, b_mod, w_in, w_out, lru_conv_w, lru_conv_b, lru_wa, lru_ba,
           lru_wx, lru_bx, lru_lam, s5_lam_re, s5_lam_im, s5_log_dt, s5_b_re, s5_b_im, s5_c_re, s5_c_im,
           s5_d, s5_w_glu, s5_b_glu, da_lam, da_norm_g, final_g):
    bsz, t_len, d = x.shape
    n_ctx = ctx.shape[1]
    s_len = n_ctx + t_len
    n_rows = s_len * bsz
    n_layers = w_in.shape[0]
    assert bsz == BATCH and bsz + 1 <= MOD_ROWS
    assert n_ctx % Q_TILE == 0 and t_len % Q_TILE == 0
    assert n_ctx % TOKEN_TILE == 0 and n_ctx % SCAN_TILE == 0

    cc = jnp.concatenate([c, c_ctx[None], jnp.zeros((MOD_ROWS - bsz - 1, d), F32)], axis=0)
    mod = _modulation(cc, w_mod.astype(BF16), b_mod).reshape(n_layers, MOD_ROWS, 3, d)
    mod_lat = mod[:, :bsz].transpose(0, 2, 1, 3)
    mod_ctx = jnp.broadcast_to(mod[:, bsz:bsz + 1], (n_layers, bsz, 3, d)).transpose(0, 2, 1, 3)
    mods = jnp.stack([mod_ctx, mod_lat], axis=1)

    tables = _rope_tables()

    side = W_LRU + W_S5 + 2 * W_DA
    c_ua, c_us = (0, W_LRU), (W_LRU, W_LRU + W_S5)
    c_k, c_v = (W_LRU + W_S5, W_LRU + W_S5 + W_DA), (W_LRU + W_S5 + W_DA, side)
    sl = lambda lo_hi, base=0: w_in[:, :, base + lo_hi[0]:base + lo_hi[1]]
    wf = jnp.concatenate([sl(c_ua), sl(c_us), sl(c_ua, side), sl(c_us, side), sl(c_v, side)],
                         axis=-1).astype(BF16)
    wa = jnp.concatenate([sl(c_k), sl(c_v), sl(c_k, side)], axis=-1).astype(BF16)
    w_out_b = w_out.astype(BF16)
    w_glu_b = s5_w_glu.astype(BF16)

    bd = lambda w: _block_diag(w.reshape((-1,) + w.shape[-3:])).reshape(
        w.shape[:-3] + (w.shape[-3] * w.shape[-2], w.shape[-3] * w.shape[-1]))
    wg = jnp.concatenate([bd(lru_wa), bd(lru_wx)], axis=-1).astype(BF16)
    bg = jnp.concatenate([lru_ba, lru_bx], axis=-1).reshape(n_layers, 2, 1, 2 * W_LRU)
    lru_lam_r = lru_lam.reshape(n_layers, 2, 1, W_LRU)
    lru_conv_b_r = lru_conv_b.reshape(n_layers, 1, W_LRU)
    flat = lambda a: a.reshape(n_layers, 2, 1, S5_N)
    ldt = jnp.broadcast_to(s5_log_dt[..., None], (n_layers, 2, S5_G, S5_P))
    bre = bd(jnp.swapaxes(s5_b_re, -1, -2))
    bim = bd(jnp.swapaxes(s5_b_im, -1, -2))
    cre = bd(jnp.swapaxes(s5_c_re, -1, -2)).astype(BF16)
    cim = bd(jnp.swapaxes(s5_c_im, -1, -2)).astype(BF16)
    lam_re, lam_im, ldt = flat(s5_lam_re), flat(s5_lam_im), flat(ldt)
    lru_params = [(lru_conv_w[l], lru_conv_b_r[l], wg[l], bg[l], lru_lam_r[l]) for l in range(n_layers)]
    s5_params = [(lam_re[l], lam_im[l], ldt[l], bre[l], bim[l], cre[l], cim[l]) for l in range(n_layers)]

    hs, pf, kq, v = _in_proj(ctx, x, mods[0], norm_g[0], wf[0], wa[0], tables)
    for l in range(n_layers):
        final = l == n_layers - 1
        lam_init = 0.8 - 0.6 * math.exp(-0.3 * l)
        pf3 = pf.reshape(s_len, bsz, PF_W)

        hf, hb, yf, yb = _scans(pf3, lru_params[l], s5_params[l], n_ctx=n_ctx)

        yd = _attention(kq, v, da_lam[l], da_norm_g[l], n_ctx=n_ctx, lam_init=lam_init)

        mix = (hf.reshape(n_rows, W_LRU), hb.reshape(n_rows, W_LRU),
               yf.reshape(n_rows, W_S5), yb.reshape(n_rows, W_S5),
               pf, yd, hs, mods[l], s5_d[l], w_glu_b[l], s5_b_glu[l], w_out_b[l])
        if final:
            return _out_proj(*mix, n_ctx=n_ctx, final_g=final_g)
        hs, pf, kq, v = _out_proj(*mix, n_ctx=n_ctx, next_proj=(
            mods[l + 1], norm_g[l + 1], wf[l + 1], wa[l + 1], tables))
```

```python
import functools
import math

import jax
import jax.numpy as jnp
from jax import lax
from jax.experimental import pallas as pl
from jax.experimental.pallas import tpu as pltpu

F32 = jnp.float32
BF16 = jnp.bfloat16

GRID_W = 64
EPS = 1e-6
W_LRU = 256
LRU_BLOCKS = 4
CONV_W = 4
LRU_C = 8.0
W_S5 = 256
S5_H = 16
S5_G = 16
S5_P = 64
S5_N = S5_G * S5_P
DA_HEADS = 4
DA_DH = 64
DA_DV = 2 * DA_DH
W_DA = DA_HEADS * DA_DV
ROPE_BASE = 10000.0
ROPE_F = DA_DH // 4
LANES = 128
BATCH = 8

PF_W = 2 * W_LRU + 2 * W_S5 + W_DA
PA_W = 3 * W_DA

TOKEN_TILE = 64
PROJ_PARTS = 2
SCAN_TILE = 64
Q_TILE = 256
MOD_ROWS = 16
VMEM_LIMIT = 48 * 1024 * 1024


def _cparams(sem):
    return pltpu.CompilerParams(dimension_semantics=sem, vmem_limit_bytes=VMEM_LIMIT)


def _silu(x):
    return x * jax.nn.sigmoid(x)


def _softplus(x):
    return jnp.maximum(x, 0.0) + jnp.log1p(jnp.exp(-jnp.abs(x)))


def _rows_to_batch(val, scr, slab0, out_ref, col0, dtype, t0=0):
    rows, width = val.shape
    tt = rows // BATCH
    r0 = t0 * BATCH
    for j in range(width // LANES):
        scr[slab0 + j, r0:r0 + rows, :] = val[:, j * LANES:(j + 1) * LANES]
    for b in range(BATCH):
        for j in range(width // LANES):
            out_ref[b, t0:t0 + tt, col0 + j * LANES:col0 + (j + 1) * LANES] = (
                scr[slab0 + j, pl.ds(r0 + b, tt, stride=BATCH), :].astype(dtype))


def _batch_to_rows(in_ref, scr):
    _, tt, width = in_ref.shape
    for b in range(BATCH):
        for j in range(width // LANES):
            scr[j, pl.ds(b, tt, stride=BATCH), :] = in_ref[b, :, j * LANES:(j + 1) * LANES].astype(F32)
    return jnp.concatenate([scr[j] for j in range(width // LANES)], axis=1)


def _rope_table_kernel(csp_ref, snp_ref, csr_ref, snr_ref):
    def table(shape, pos_shift):
        pos = lax.broadcasted_iota(jnp.int32, shape, 0) >> pos_shift
        lane = lax.broadcasted_iota(jnp.int32, shape, 1)
        f = (lane & (ROPE_F - 1)).astype(F32)
        ang = pos.astype(F32) * jnp.exp(-math.log(ROPE_BASE) * f / ROPE_F)
        sn = jnp.sin(ang)
        return jnp.cos(ang), jnp.where((lane & ROPE_F) == 0, -sn, sn)

    csp_ref[...], snp_ref[...] = table(csp_ref.shape, 0)
    csr_ref[...], snr_ref[...] = table(csr_ref.shape, int(math.log2(BATCH)))


def _rope_tables():
    shapes = [(GRID_W, LANES)] * 2 + [(GRID_W * BATCH, LANES)] * 2
    return pl.pallas_call(
        _rope_table_kernel,
        out_shape=[jax.ShapeDtypeStruct(s, F32) for s in shapes],
        compiler_params=pltpu.CompilerParams(vmem_limit_bytes=VMEM_LIMIT),
    )()


def _mod_kernel(c_ref, w_ref, b_ref, o_ref):
    c = c_ref[...]
    o_ref[0] = jnp.dot(_silu(c).astype(BF16), w_ref[0], preferred_element_type=F32) + b_ref[0]


def _modulation(cc, w_mod, b_mod):
    n_layers, d, d3 = w_mod.shape
    return pl.pallas_call(
        _mod_kernel,
        grid=(n_layers, d3 // d),
        in_specs=[pl.BlockSpec((MOD_ROWS, d), lambda l, j: (0, 0)),
                  pl.BlockSpec((1, d, d), lambda l, j: (l, 0, j)),
                  pl.BlockSpec((1, 1, d), lambda l, j: (l, 0, j))],
        out_specs=pl.BlockSpec((1, MOD_ROWS, d), lambda l, j: (l, 0, j)),
        out_shape=jax.ShapeDtypeStruct((n_layers, MOD_ROWS, d3), F32),
        compiler_params=_cparams(("arbitrary", "arbitrary")),
    )(cc, w_mod, b_mod.reshape(n_layers, 1, d3))


def _rope(x, cos, sin):
    width = x.shape[1]
    reps = width // LANES
    cos = jnp.concatenate([cos] * reps, axis=1)
    sin = jnp.concatenate([sin] * reps, axis=1)
    lane = lax.broadcasted_iota(jnp.int32, x.shape, 1)
    first = (lane & ROPE_F) == 0
    partner = jnp.where(first, pltpu.roll(x, width - ROPE_F, 1), pltpu.roll(x, ROPE_F, 1))
    return x * cos + partner * sin


def _project(i, x, mod_ref, g_ref, wf_ref, wa_ref, csp_ref, snp_ref, csr_ref, snr_ref,
             pf_ref, kq_ref, v_ref, scr, ncb):
    rows_all, d = x.shape
    rows = rows_all // PROJ_PARTS
    tt = rows // BATCH
    shift = mod_ref[0, 0]
    scale = mod_ref[0, 1]
    grid_row = jnp.maximum(i - ncb, 0)
    is_ctx = i < ncb
    lane = lax.broadcasted_iota(jnp.int32, (rows, LANES), 1)
    use_col = (lane & (DA_DH // 2)) != 0
    slabs = W_DA // LANES
    for part in range(PROJ_PARTS):
        r0 = part * rows
        xp = x[r0:r0 + rows]
        y = xp * lax.rsqrt(jnp.mean(xp * xp, axis=-1, keepdims=True) + EPS) * g_ref[...]
        y3 = y.reshape(tt, BATCH, d)
        n = (y3 * (1.0 + scale)[None] + shift[None]).reshape(rows, d).astype(BF16)
        pf_ref[r0:r0 + rows, :] = jnp.dot(n, wf_ref[...], preferred_element_type=F32)
        att = jnp.dot(n, wa_ref[...], preferred_element_type=F32)
        cos = jnp.where(use_col, csr_ref[r0:r0 + rows, :], csp_ref[pl.ds(grid_row, 1), :])
        sin = jnp.where(use_col, snr_ref[r0:r0 + rows, :], snp_ref[pl.ds(grid_row, 1), :])
        cos = jnp.where(is_ctx, 1.0, cos)
        sin = jnp.where(is_ctx, 0.0, sin)
        k = _rope(att[:, :W_DA], cos, sin)
        v = att[:, W_DA:2 * W_DA]
        q = _rope(att[:, 2 * W_DA:], cos, sin) * (DA_DH ** -0.5 * math.log2(math.e))
        _rows_to_batch(k, scr, 0, kq_ref, 0, BF16, part * tt)
        _rows_to_batch(q, scr, slabs, kq_ref, W_DA, BF16, part * tt)
        _rows_to_batch(v, scr, 2 * slabs, v_ref, 0, BF16, part * tt)


def _in_proj_kernel(ctx_ref, x_ref, *refs, ncb):
    proj_in, (hs_ref, pf_ref, kq_ref, v_ref, scr, rows_scr) = refs[:8], refs[8:]
    i = pl.program_id(0)

    @pl.when(i < ncb)
    def _():
        hs_ref[...] = _batch_to_rows(ctx_ref, rows_scr)

    @pl.when(i >= ncb)
    def _():
        hs_ref[...] = _batch_to_rows(x_ref, rows_scr)

    _project(i, hs_ref[...], *proj_in, pf_ref, kq_ref, v_ref, scr, ncb)


def _project_specs(d, tt, n_rows, kind, tables):
    tm = tt * BATCH
    s_len = n_rows // BATCH
    const = lambda shape: pl.BlockSpec(shape, lambda i: (0,) * len(shape),
                                       pipeline_mode=pl.Buffered(1))
    in_specs = [pl.BlockSpec((1, 3, BATCH, d), lambda i: (kind(i), 0, 0, 0)),
                const((1, d)), const((d, PF_W)), const((d, PA_W))] + [const(t.shape) for t in tables]
    out_specs = [pl.BlockSpec((tm, PF_W), lambda i: (i, 0)),
                 pl.BlockSpec((BATCH, tt, 2 * W_DA), lambda i: (0, i, 0)),
                 pl.BlockSpec((BATCH, tt, W_DA), lambda i: (0, i, 0))]
    out_shape = [jax.ShapeDtypeStruct((n_rows, PF_W), F32),
                 jax.ShapeDtypeStruct((BATCH, s_len, 2 * W_DA), BF16),
                 jax.ShapeDtypeStruct((BATCH, s_len, W_DA), BF16)]
    scratch = pltpu.VMEM((PA_W // LANES, tm, LANES), F32)
    return in_specs, out_specs, out_shape, scratch


def _in_proj(ctx, x, mods, g, wf, wa, tables):
    bsz, n_ctx, d = ctx.shape
    tt = TOKEN_TILE
    assert tt == GRID_W
    tm = tt * BATCH
    ncb = n_ctx // tt
    n = ncb + x.shape[1] // tt
    kind = lambda i: (i >= ncb).astype(jnp.int32)
    in_specs, out_specs, out_shape, scratch = _project_specs(d, tt, n * tm, kind, tables)
    return pl.pallas_call(
        functools.partial(_in_proj_kernel, ncb=ncb),
        grid=(n,),
        in_specs=[pl.BlockSpec((bsz, tt, d), lambda i: (0, jnp.minimum(i, ncb - 1), 0)),
                  pl.BlockSpec((bsz, tt, d), lambda i: (0, jnp.maximum(i - ncb, 0), 0))] + in_specs,
        out_specs=[pl.BlockSpec((tm, d), lambda i: (i, 0))] + out_specs,
        out_shape=[jax.ShapeDtypeStruct((n * tm, d), F32)] + out_shape,
        scratch_shapes=[scratch, pltpu.VMEM((d // LANES, tm, LANES), F32)],
        compiler_params=_cparams(("arbitrary",)),
    )(ctx, x, mods, g.reshape(1, d), wf, wa, *tables)


def _bwd_chunk(i, ncc, n):
    return jnp.where(i < ncc, ncc - 1 - i, n - 1 - i + ncc)


S5_LANE_CHUNK = 512
N_LRU_IN, N_S5_IN = 11, 9


def _lru_gates(d, c, u_c, u_p, u_n, cw_ref, cb_ref, wg_ref, bg_ref, lam_ref, a_scr, b_scr, ts, ncc, n):
    seg_start = jnp.logical_or(c == 0, c == ncc)
    seg_end = jnp.logical_or(c == ncc - 1, c == n - 1)
    prev = u_p[...] * jnp.where(seg_start, 0.0, 1.0)
    nxt = u_n[...] * jnp.where(seg_end, 0.0, 1.0)
    u = jnp.concatenate([prev, u_c[...], nxt], axis=0)
    cw = cw_ref[...]
    xc = cb_ref[...] + cw[0:1] * u[0:ts]
    for j in range(1, CONV_W):
        xc = xc + cw[j:j + 1] * u[j:j + ts]
    xc2 = xc.reshape(ts * xc.shape[1], W_LRU)
    gts = jnp.dot(xc2.astype(BF16), wg_ref[d], preferred_element_type=F32) + bg_ref[d]
    gate_r = jax.nn.sigmoid(gts[:, :W_LRU])
    gate_i = jax.nn.sigmoid(gts[:, W_LRU:])
    log_a = -LRU_C * gate_r * _softplus(-lam_ref[d])
    a = jnp.exp(log_a)
    th = jnp.tanh(log_a)
    one_minus_a2 = -2.0 * th / (1.0 - th)
    bb = jnp.sqrt(one_minus_a2) * (gate_i * xc2)
    a_scr[d] = a.reshape(xc.shape)
    b_scr[d] = bb.reshape(xc.shape)


def _s5_discretise(lre_ref, lim_ref, ldt_ref, bre_ref, bim_ref, bbar_scr, a_scr):
    bsz, nst = a_scr.shape[2:]
    for d in range(2):
        lre = lre_ref[d]
        lim = lim_ref[d]
        dt = jnp.exp(ldt_ref[d])
        mag = jnp.exp(lre * dt)
        ang = lim * dt
        ar = mag * jnp.cos(ang)
        ai = mag * jnp.sin(ang)
        nr = ar - 1.0
        den = lre * lre + lim * lim
        cfr = (nr * lre + ai * lim) / den
        cfi = (ai * lre - nr * lim) / den
        bre = bre_ref[d]
        bim = bim_ref[d]
        bbar_scr[d, :, :nst] = (cfr * bre - cfi * bim).astype(BF16)
        bbar_scr[d, :, nst:] = (cfr * bim + cfi * bre).astype(BF16)
        a_scr[d, 0] = jnp.broadcast_to(ar, (bsz, nst))
        a_scr[d, 1] = jnp.broadcast_to(ai, (bsz, nst))


def _scan_kernel(*refs, ts, ncc, n):
    lru_in, s5_in = refs[:N_LRU_IN], refs[N_LRU_IN:N_LRU_IN + N_S5_IN]
    hf_ref, hb_ref, yf_ref, yb_ref = refs[N_LRU_IN + N_S5_IN:N_LRU_IN + N_S5_IN + 4]
    la_scr, lb_scr, lcarry_scr, bbar_scr, sa_scr, buf_scr, scarry_scr = refs[N_LRU_IN + N_S5_IN + 4:]
    ufc, ufp, ufn, ubc, ubp, ubn, cw_ref, cb_ref, wg_ref, bg_ref, lam_ref = lru_in
    uf_ref, ub_ref, lre_ref, lim_ref, ldt_ref, bre_ref, bim_ref, cre_ref, cim_ref = s5_in
    i = pl.program_id(0)
    nst = S5_N
    bsz = uf_ref.shape[1]

    @pl.when(i == 0)
    def _():
        lcarry_scr[...] = jnp.zeros_like(lcarry_scr)
        scarry_scr[...] = jnp.zeros_like(scarry_scr)
        _s5_discretise(lre_ref, lim_ref, ldt_ref, bre_ref, bim_ref, bbar_scr, sa_scr)

    for d, u_ref in enumerate((uf_ref, ub_ref)):
        u2 = u_ref[...].reshape(ts * bsz, W_S5).astype(BF16)
        bu = jnp.dot(u2, bbar_scr[d], preferred_element_type=F32)
        buf_scr[d] = bu.reshape(ts, bsz, 2 * nst)

    lru_par = (cw_ref, cb_ref, wg_ref, bg_ref, lam_ref, la_scr, lb_scr, ts, ncc, n)
    _lru_gates(0, i, ufc, ufp, ufn, *lru_par)
    _lru_gates(1, _bwd_chunk(i, ncc, n), ubc, ubp, ubn, *lru_par)
    hf = lcarry_scr[0]
    hb = lcarry_scr[1]
    for k in range(ts):
        kb = ts - 1 - k
        hf = la_scr[0, k] * hf + lb_scr[0, k]
        hb = la_scr[1, kb] * hb + lb_scr[1, kb]
        hf_ref[k] = hf
        hb_ref[kb] = hb
    lcarry_scr[0] = hf
    lcarry_scr[1] = hb

    lc = S5_LANE_CHUNK
    chains = [(d, pl.ds(c0, lc), pl.ds(nst + c0, lc)) for d in range(2) for c0 in range(0, nst, lc)]
    state = [(scarry_scr[d, :, re_sl], scarry_scr[d, :, im_sl]) for d, re_sl, im_sl in chains]
    for k in range(ts):
        for n_chain, (d, re_sl, im_sl) in enumerate(chains):
            t = k if d == 0 else ts - 1 - k
            ar = sa_scr[d, 0, :, re_sl]
            ai = sa_scr[d, 1, :, re_sl]
            sre, sim = state[n_chain]
            nre = ar * sre - ai * sim + buf_scr[d, t, :, re_sl]
            nim = ar * sim + ai * sre + buf_scr[d, t, :, im_sl]
            buf_scr[d, t, :, re_sl] = nre
            buf_scr[d, t, :, im_sl] = nim
            state[n_chain] = (nre, nim)
    for (d, re_sl, im_sl), (sre, sim) in zip(chains, state):
        scarry_scr[d, :, re_sl] = sre
        scarry_scr[d, :, im_sl] = sim

    for d, y_ref in enumerate((yf_ref, yb_ref)):
        st = buf_scr[d].reshape(ts * bsz, 2 * nst)
        y = (jnp.dot(st[:, :nst].astype(BF16), cre_ref[d], preferred_element_type=F32)
             - jnp.dot(st[:, nst:].astype(BF16), cim_ref[d], preferred_element_type=F32))
        y_ref[...] = y.reshape(ts, bsz, W_S5)


def _scans(pf3, lru_params, s5_params, *, n_ctx):
    s_len, bsz, _ = pf3.shape
    ts = SCAN_TILE
    n = s_len // ts
    ncc = n_ctx // ts
    fwd = lambda i: i
    bwd = lambda i: _bwd_chunk(i, ncc, n)
    s5_col = W_LRU // W_S5

    def cur(cfn, col=0):
        return pl.BlockSpec((ts, bsz, W_LRU), lambda i: (cfn(i), 0, col))

    def prev(cfn):
        return pl.BlockSpec((2, bsz, W_LRU), lambda i: (jnp.maximum(cfn(i) * (ts // 2) - 1, 0), 0, 0))

    def nxt(cfn):
        return pl.BlockSpec((1, bsz, W_LRU), lambda i: (jnp.minimum((cfn(i) + 1) * ts, s_len - 1), 0, 0))

    const = lambda a: pl.BlockSpec(a.shape, lambda i: (0,) * a.ndim, pipeline_mode=pl.Buffered(1))
    assert len(lru_params) == N_LRU_IN - 6 and len(s5_params) == N_S5_IN - 2
    out_block = lambda cfn: pl.BlockSpec((ts, bsz, W_LRU), lambda i: (cfn(i), 0, 0))
    return pl.pallas_call(
        functools.partial(_scan_kernel, ts=ts, ncc=ncc, n=n),
        grid=(n,),
        in_specs=([cur(fwd), prev(fwd), nxt(fwd), cur(bwd), prev(bwd), nxt(bwd)]
                  + [const(a) for a in lru_params]
                  + [cur(fwd, s5_col), cur(bwd, s5_col)] + [const(a) for a in s5_params]),
        out_specs=[out_block(fwd), out_block(bwd), out_block(fwd), out_block(bwd)],
        out_shape=[jax.ShapeDtypeStruct((s_len, bsz, W_LRU), F32)] * 4,
        scratch_shapes=[pltpu.VMEM((2, ts, bsz, W_LRU), F32),
                        pltpu.VMEM((2, ts, bsz, W_LRU), F32),
                        pltpu.VMEM((2, bsz, W_LRU), F32),
                        pltpu.VMEM((2, W_S5, 2 * S5_N), BF16),
                        pltpu.VMEM((2, 2, bsz, S5_N), F32),
                        pltpu.VMEM((2, ts, bsz, 2 * S5_N), F32),
                        pltpu.VMEM((2, bsz, 2 * S5_N), F32)],
        compiler_params=_cparams(("arbitrary",)),
    )(*([pf3] * 6), *lru_params, pf3, pf3, *s5_params)


V_ROWS = DA_DV + 16
SCORE_PARTS = 2
SCORE_SLOTS = 4


def _attn_kernel(q_ref, k_ref, v_ref, dl_ref, g_ref, o_ref, s_scr, vt_scr, *, tq, n_ctx, lam_init):
    s_len = k_ref.shape[1]
    nq = s_len // tq
    ncq = n_ctx // tq
    for c0 in range(0, s_len, tq):
        vt_scr[0:DA_DV, c0:c0 + tq] = v_ref[0, c0:c0 + tq, :].astype(F32).T.astype(BF16)
    pad_row = lax.broadcasted_iota(jnp.int32, (V_ROWS - DA_DV, s_len), 0)
    vt_scr[DA_DV:, :] = jnp.where(pad_row == 0, 1.0, 0.0).astype(BF16)
    dl = dl_ref[...]
    lam = (jnp.exp(jnp.sum(dl[0:1] * dl[1:2], axis=-1, keepdims=True))
           - jnp.exp(jnp.sum(dl[2:3] * dl[3:4], axis=-1, keepdims=True)) + lam_init)
    gscale = g_ref[...] * (1.0 - lam_init)
    lane = lax.broadcasted_iota(jnp.int32, (tq, LANES), 1)

    def rows(qt):
        return pl.ds(pl.multiple_of(qt * tq, tq), tq)

    def key_parts(nk, max_parts):
        groups = nk // LANES
        n_parts = min(max_parts, groups)
        bounds = [LANES * ((groups * p) // n_parts) for p in range(n_parts + 1)]
        return tuple(zip(bounds[:-1], bounds[1:]))

    def scores(qt, mp, nk, slot):
        q = q_ref[0, rows(qt), :]
        qm = jnp.where((lane < DA_DH) if mp == 0 else (lane >= DA_DH), q, jnp.zeros_like(q))
        m = None
        for lo, hi in key_parts(nk, SCORE_PARTS):
            s_t = lax.dot_general(k_ref[0, lo:hi, :], qm, (((1,), (1,)), ((), ())),
                                  preferred_element_type=F32)
            s_scr[slot, lo:hi, :] = s_t
            m_part = jnp.max(s_t, axis=0, keepdims=True)
            m = m_part if m is None else jnp.maximum(m, m_part)
        return m

    def values(slot, m, nk):
        p_t = jnp.exp2(s_scr[slot, 0:nk, :] - m).astype(BF16)
        return jnp.dot(vt_scr[:, 0:nk], p_t, preferred_element_type=F32)

    def finish(qt, o1, o2):
        o_t = (o1[:DA_DV] * (1.0 / o1[DA_DV:DA_DV + 1])
               - o2[:DA_DV] * (lam / o2[DA_DV:DA_DV + 1]))
        y_t = o_t * lax.rsqrt(jnp.mean(o_t * o_t, axis=0, keepdims=True) + EPS)
        o_ref[0, rows(qt), :] = y_t.T * gscale

    for qt in range(ncq):
        m0 = scores(qt, 0, n_ctx, 0)
        m1 = scores(qt, 1, n_ctx, 2)
        finish(qt, values(0, m0, n_ctx), values(2, m1, n_ctx))

    def pair(t0, m0, last):
        m1 = scores(t0, 1, s_len, 2)
        o1 = values(0, m0, s_len)
        m0b = scores(t0 + 1, 0, s_len, 1)
        o2 = values(2, m1, s_len)
        finish(t0, o1, o2)
        m1b = scores(t0 + 1, 1, s_len, 3)
        o1b = values(1, m0b, s_len)
        m0_next = None if last else scores(t0 + 2, 0, s_len, 0)
        o2b = values(3, m1b, s_len)
        finish(t0 + 1, o1b, o2b)
        return m0_next

    n_pairs = (nq - ncq) // 2
    m0 = lax.fori_loop(0, n_pairs - 1, lambda j, m0: pair(ncq + 2 * j, m0, False),
                       scores(ncq, 0, s_len, 0))
    pair(nq - 2, m0, True)


def _attention(qk, v, da_lam, da_g, *, n_ctx, lam_init):
    bsz, s_len, _ = qk.shape
    tq = Q_TILE
    assert n_ctx % tq == 0 and (s_len - n_ctx) % (2 * tq) == 0
    return pl.pallas_call(
        functools.partial(_attn_kernel, tq=tq, n_ctx=n_ctx, lam_init=lam_init),
        grid=(bsz, DA_HEADS),
        in_specs=[pl.BlockSpec((1, s_len, LANES), lambda b, h: (b, 0, DA_HEADS + h)),
                  pl.BlockSpec((1, s_len, LANES), lambda b, h: (b, 0, h)),
                  pl.BlockSpec((1, s_len, LANES), lambda b, h: (b, 0, h)),
                  pl.BlockSpec((4, DA_DH), lambda b, h: (0, 0)),
                  pl.BlockSpec((1, DA_DV), lambda b, h: (0, 0))],
        out_specs=pl.BlockSpec((1, s_len, LANES), lambda b, h: (b, 0, h)),
        out_shape=jax.ShapeDtypeStruct((bsz, s_len, W_DA), F32),
        scratch_shapes=[pltpu.VMEM((SCORE_SLOTS, s_len, tq), F32),
                        pltpu.VMEM((V_ROWS, s_len), BF16)],
        compiler_params=_cparams(("arbitrary", "arbitrary")),
    )(qk, qk, v, da_lam.reshape(4, DA_DH), da_g.reshape(1, DA_DV))


def _gelu_tanh(x):
    return 0.5 * x * (1.0 + jnp.tanh(math.sqrt(2.0 / math.pi) * (x + 0.044715 * (x * x * x))))


N_MIX_REFS = 15


def _out_proj_kernel(*refs, final, ncb):
    (hf_ref, hb_ref, yf_ref, yb_ref, us_ref, ga_ref, gs_ref, gd_ref, yd_ref, h_ref, mod_ref,
     d_ref, wglu_ref, bglu_ref, wo_ref) = refs[:N_MIX_REFS]
    yd_scr = refs[-1]
    y_a = (hf_ref[...] + hb_ref[...]) * _silu(ga_ref[...])
    z = _gelu_tanh(d_ref[...] * us_ref[...] + yf_ref[...] + yb_ref[...])
    glu = jax.nn.sigmoid(jnp.dot(z.astype(BF16), wglu_ref[...], preferred_element_type=F32)
                         + bglu_ref[...])
    y_s = z * glu * _silu(gs_ref[...])
    y_d = _batch_to_rows(yd_ref, yd_scr) * _silu(gd_ref[...])
    out = (jnp.dot(y_a.astype(BF16), wo_ref[0:W_LRU, :], preferred_element_type=F32)
           + jnp.dot(y_s.astype(BF16), wo_ref[W_LRU:W_LRU + W_S5, :], preferred_element_type=F32)
           + jnp.dot(y_d.astype(BF16), wo_ref[W_LRU + W_S5:, :], preferred_element_type=F32))
    rows, d = out.shape
    gate = mod_ref[0, 2]
    hn = h_ref[...] + (out.reshape(rows // BATCH, BATCH, d) * gate[None]).reshape(rows, d)
    if final:
        fg_ref, o_ref, out_scr = refs[N_MIX_REFS:-1]
        hn = hn * lax.rsqrt(jnp.mean(hn * hn, axis=-1, keepdims=True) + EPS) * fg_ref[...]
        _rows_to_batch(hn, out_scr, 0, o_ref, 0, F32)
    else:
        n_proj_in = 8
        hs_ref = refs[N_MIX_REFS + n_proj_in]
        hs_ref[...] = hn
        _project(pl.program_id(0), hn, *refs[N_MIX_REFS:N_MIX_REFS + n_proj_in],
                 *refs[N_MIX_REFS + n_proj_in + 1:-1], ncb)


def _out_proj(hf, hb, yf, yb, pf, yd, hs, mods, s5_d, w_glu, b_glu, w_out, *, n_ctx,
              final_g=None, next_proj=None):
    final = final_g is not None
    n_rows, d = hs.shape
    tt = TOKEN_TILE
    tm = tt * BATCH
    ncb = n_ctx // tt
    off = ncb if final else 0
    nblk = n_rows // tm - off
    blk = lambda w, c=0: pl.BlockSpec((tm, w), lambda i: (i + off, c))
    const = lambda a: pl.BlockSpec(a.shape, lambda i: (0,) * a.ndim, pipeline_mode=pl.Buffered(1))
    kind = lambda i: (i + off >= ncb).astype(jnp.int32)
    s5_d = s5_d.reshape(1, W_S5)
    b_glu = b_glu.reshape(1, W_S5)
    in_specs = [blk(W_LRU), blk(W_LRU), blk(W_S5), blk(W_S5),
                blk(W_S5, 1), blk(W_LRU, 2), blk(W_S5, 3), blk(W_DA, 2),
                pl.BlockSpec((BATCH, tt, W_DA), lambda i: (0, i + off, 0)), blk(d),
                pl.BlockSpec((1, 3, BATCH, d), lambda i: (kind(i), 0, 0, 0)),
                const(s5_d), const(w_glu), const(b_glu), const(w_out)]
    operands = [hf, hb, yf, yb, pf, pf, pf, pf, yd, hs, mods, s5_d, w_glu, b_glu, w_out]
    assert len(operands) == N_MIX_REFS
    yd_scratch = pltpu.VMEM((W_DA // LANES, tm, LANES), F32)
    if final:
        final_g = final_g.reshape(1, d)
        in_specs += [const(final_g)]
        operands += [final_g]
        out_specs = pl.BlockSpec((BATCH, tt, d), lambda i: (0, i, 0))
        out_shape = jax.ShapeDtypeStruct((BATCH, nblk * tt, d), F32)
        scratch = [pltpu.VMEM((d // LANES, tm, LANES), F32), yd_scratch]
    else:
        mods2, g2, wf, wa, tables = next_proj
        p_in, p_out, p_shape, p_scratch = _project_specs(d, tt, n_rows, kind, tables)
        in_specs += p_in
        operands += [mods2, g2.reshape(1, d), wf, wa, *tables]
        out_specs = [pl.BlockSpec((tm, d), lambda i: (i, 0))] + p_out
        out_shape = [jax.ShapeDtypeStruct((n_rows, d), F32)] + p_shape
        scratch = [p_scratch, yd_scratch]
    return pl.pallas_call(
        functools.partial(_out_proj_kernel, final=final, ncb=ncb),
        grid=(nblk,),
        in_specs=in_specs,
        out_specs=out_specs,
        out_shape=out_shape,
        scratch_shapes=scratch,
        compiler_params=_cparams(("arbitrary",)),
    )(*operands)


def _block_diag(w):
    m, nb, k, n = w.shape
    eye = jnp.eye(nb, dtype=w.dtype)
    return jnp.einsum('bnkd,nm->bnkmd', w, eye).reshape(m, nb * k, nb * n)


def kernel(x, c, ctx, c_ctx, norm_g, w_mod, b_mod, w_in, w_out, lru_conv_w, lru_conv_b, lru_wa, lru_ba,
           lru_wx, lru_bx, lru_lam, s5_lam_re, s5_lam_im, s5_log_dt, s5_b_re, s5_b_im, s5_c_re, s5_c_im,
           s5_d, s5_w_glu, s5_b_glu, da_lam, da_norm_g, final_g):
    bsz, t_len, d = x.shape
    n_ctx = ctx.shape[1]
    s_len = n_ctx + t_len
    n_rows = s_len * bsz
    n_layers = w_in.shape[0]
    assert bsz == BATCH and bsz + 1 <= MOD_ROWS
    assert n_ctx % Q_TILE == 0 and t_len % Q_TILE == 0
    assert n_ctx % TOKEN_TILE == 0 and n_ctx % SCAN_TILE == 0

    cc = jnp.concatenate([c, c_ctx[None], jnp.zeros((MOD_ROWS - bsz - 1, d), F32)], axis=0)
    mod = _modulation(cc, w_mod.astype(BF16), b_mod).reshape(n_layers, MOD_ROWS, 3, d)
    mod_lat = mod[:, :bsz].transpose(0, 2, 1, 3)
    mod_ctx = jnp.broadcast_to(mod[:, bsz:bsz + 1], (n_layers, bsz, 3, d)).transpose(0, 2, 1, 3)
    mods = jnp.stack([mod_ctx, mod_lat], axis=1)

    tables = _rope_tables()

    side = W_LRU + W_S5 + 2 * W_DA
    c_ua, c_us = (0, W_LRU), (W_LRU, W_LRU + W_S5)
    c_k, c_v = (W_LRU + W_S5, W_LRU + W_S5 + W_DA), (W_LRU + W_S5 + W_DA, side)
    sl = lambda lo_hi, base=0: w_in[:, :, base + lo_hi[0]:base + lo_hi[1]]
    wf = jnp.concatenate([sl(c_ua), sl(c_us), sl(c_ua, side), sl(c_us, side), sl(c_v, side)],
                         axis=-1).astype(BF16)
    wa = jnp.concatenate([sl(c_k), sl(c_v), sl(c_k, side)], axis=-1).astype(BF16)
    w_out_b = w_out.astype(BF16)
    w_glu_b = s5_w_glu.astype(BF16)

    bd = lambda w: _block_diag(w.reshape((-1,) + w.shape[-3:])).reshape(
        w.shape[:-3] + (w.shape[-3] * w.shape[-2], w.shape[-3] * w.shape[-1]))
    wg = jnp.concatenate([bd(lru_wa), bd(lru_wx)], axis=-1).astype(BF16)
    bg = jnp.concatenate([lru_ba, lru_bx], axis=-1).reshape(n_layers, 2, 1, 2 * W_LRU)
    lru_lam_r = lru_lam.reshape(n_layers, 2, 1, W_LRU)
    lru_conv_b_r = lru_conv_b.reshape(n_layers, 1, W_LRU)
    flat = lambda a: a.reshape(n_layers, 2, 1, S5_N)
    ldt = jnp.broadcast_to(s5_log_dt[..., None], (n_layers, 2, S5_G, S5_P))
    bre = bd(jnp.swapaxes(s5_b_re, -1, -2))
    bim = bd(jnp.swapaxes(s5_b_im, -1, -2))
    cre = bd(jnp.swapaxes(s5_c_re, -1, -2)).astype(BF16)
    cim = bd(jnp.swapaxes(s5_c_im, -1, -2)).astype(BF16)
    lam_re, lam_im, ldt = flat(s5_lam_re), flat(s5_lam_im), flat(ldt)
    lru_params = [(lru_conv_w[l], lru_conv_b_r[l], wg[l], bg[l], lru_lam_r[l]) for l in range(n_layers)]
    s5_params = [(lam_re[l], lam_im[l], ldt[l], bre[l], bim[l], cre[l], cim[l]) for l in range(n_layers)]

    hs, pf, kq, v = _in_proj(ctx, x, mods[0], norm_g[0], wf[0], wa[0], tables)
    for l in range(n_layers):
        final = l == n_layers - 1
        lam_init = 0.8 - 0.6 * math.exp(-0.3 * l)
        pf3 = pf.reshape(s_len, bsz, PF_W)

        hf, hb, yf, yb = _scans(pf3, lru_params[l], s5_params[l], n_ctx=n_ctx)

        yd = _attention(kq, v, da_lam[l], da_norm_g[l], n_ctx=n_ctx, lam_init=lam_init)

        mix = (hf.reshape(n_rows, W_LRU), hb.reshape(n_rows, W_LRU),
               yf.reshape(n_rows, W_S5), yb.reshape(n_rows, W_S5),
               pf, yd, hs, mods[l], s5_d[l], w_glu_b[l], s5_b_glu[l], w_out_b[l])
        if final:
            return _out_proj(*mix, n_ctx=n_ctx, final_g=final_g)
        hs, pf, kq, v = _out_proj(*mix, n_ctx=n_ctx, next_proj=(
            mods[l + 1], norm_g[l + 1], wf[l + 1], wa[l + 1], tables))
```

```python
import functools
import math

import jax
import jax.numpy as jnp
from jax import lax
from jax.experimental import pallas as pl
from jax.experimental.pallas import tpu as pltpu

F32 = jnp.float32
BF16 = jnp.bfloat16

GRID_W = 64
EPS = 1e-6
W_LRU = 256
LRU_BLOCKS = 4
CONV_W = 4
LRU_C = 8.0
W_S5 = 256
S5_H = 16
S5_G = 16
S5_P = 64
S5_N = S5_G * S5_P
DA_HEADS = 4
DA_DH = 64
DA_DV = 2 * DA_DH
W_DA = DA_HEADS * DA_DV
ROPE_BASE = 10000.0
ROPE_F = DA_DH // 4
LANES = 128
BATCH = 8

PF_W = 2 * W_LRU + 2 * W_S5 + W_DA
PA_W = 3 * W_DA

TOKEN_TILE = 64
PROJ_PARTS = 2
SCAN_TILE = 64
Q_TILE = 256
MOD_ROWS = 16
VMEM_LIMIT = 48 * 1024 * 1024


def _layer_block(a, layer):
    index = (layer,) + (0,) * (a.ndim - 1)
    return pl.BlockSpec((1,) + a.shape[1:], lambda *_: index, pipeline_mode=pl.Buffered(1))


def _cparams(sem):
    return pltpu.CompilerParams(dimension_semantics=sem, vmem_limit_bytes=VMEM_LIMIT)


def _silu(x):
    return x * jax.nn.sigmoid(x)


def _softplus(x):
    return jnp.maximum(x, 0.0) + jnp.log1p(jnp.exp(-jnp.abs(x)))


def _rows_to_batch(val, scr, slab0, out_ref, col0, dtype, t0=0):
    rows, width = val.shape
    tt = rows // BATCH
    r0 = t0 * BATCH
    for j in range(width // LANES):
        scr[slab0 + j, r0:r0 + rows, :] = val[:, j * LANES:(j + 1) * LANES]
    for b in range(BATCH):
        for j in range(width // LANES):
            out_ref[b, t0:t0 + tt, col0 + j * LANES:col0 + (j + 1) * LANES] = (
                scr[slab0 + j, pl.ds(r0 + b, tt, stride=BATCH), :].astype(dtype))


def _batch_to_rows(in_ref, scr):
    _, tt, width = in_ref.shape
    for b in range(BATCH):
        for j in range(width // LANES):
            scr[j, pl.ds(b, tt, stride=BATCH), :] = in_ref[b, :, j * LANES:(j + 1) * LANES].astype(F32)
    return jnp.concatenate([scr[j] for j in range(width // LANES)], axis=1)


def _rope_table_kernel(csp_ref, snp_ref, csr_ref, snr_ref):
    def table(shape, pos_shift):
        pos = lax.broadcasted_iota(jnp.int32, shape, 0) >> pos_shift
        lane = lax.broadcasted_iota(jnp.int32, shape, 1)
        f = (lane & (ROPE_F - 1)).astype(F32)
        ang = pos.astype(F32) * jnp.exp(-math.log(ROPE_BASE) * f / ROPE_F)
        sn = jnp.sin(ang)
        return jnp.cos(ang), jnp.where((lane & ROPE_F) == 0, -sn, sn)

    csp_ref[...], snp_ref[...] = table(csp_ref.shape, 0)
    csr_ref[...], snr_ref[...] = table(csr_ref.shape, int(math.log2(BATCH)))


def _rope_tables():
    shapes = [(GRID_W, LANES)] * 2 + [(GRID_W * BATCH, LANES)] * 2
    return pl.pallas_call(
        _rope_table_kernel,
        out_shape=[jax.ShapeDtypeStruct(s, F32) for s in shapes],
        compiler_params=pltpu.CompilerParams(vmem_limit_bytes=VMEM_LIMIT),
    )()


def _mod_kernel(c_ref, w_ref, b_ref, o_ref):
    c = c_ref[...]
    o_ref[0] = jnp.dot(_silu(c).astype(BF16), w_ref[0], preferred_element_type=F32) + b_ref[0]


def _modulation(cc, w_mod, b_mod):
    n_layers, d, d3 = w_mod.shape
    return pl.pallas_call(
        _mod_kernel,
        grid=(n_layers, d3 // d),
        in_specs=[pl.BlockSpec((MOD_ROWS, d), lambda l, j: (0, 0)),
                  pl.BlockSpec((1, d, d), lambda l, j: (l, 0, j)),
                  pl.BlockSpec((1, 1, d), lambda l, j: (l, 0, j))],
        out_specs=pl.BlockSpec((1, MOD_ROWS, d), lambda l, j: (l, 0, j)),
        out_shape=jax.ShapeDtypeStruct((n_layers, MOD_ROWS, d3), F32),
        compiler_params=_cparams(("arbitrary", "arbitrary")),
    )(cc, w_mod, b_mod.reshape(n_layers, 1, d3))


def _rope(x, cos, sin):
    width = x.shape[1]
    reps = width // LANES
    cos = jnp.concatenate([cos] * reps, axis=1)
    sin = jnp.concatenate([sin] * reps, axis=1)
    lane = lax.broadcasted_iota(jnp.int32, x.shape, 1)
    first = (lane & ROPE_F) == 0
    partner = jnp.where(first, pltpu.roll(x, width - ROPE_F, 1), pltpu.roll(x, ROPE_F, 1))
    return x * cos + partner * sin


def _project(i, x, mod_ref, g_ref, wf_ref, wa_ref, csp_ref, snp_ref, csr_ref, snr_ref,
             pf_ref, kq_ref, v_ref, scr, ncb):
    rows_all, d = x.shape
    rows = rows_all // PROJ_PARTS
    tt = rows // BATCH
    shift = mod_ref[0, 0, 0]
    scale = mod_ref[0, 0, 1]
    grid_row = jnp.maximum(i - ncb, 0)
    is_ctx = i < ncb
    lane = lax.broadcasted_iota(jnp.int32, (rows, LANES), 1)
    use_col = (lane & (DA_DH // 2)) != 0
    slabs = W_DA // LANES
    for part in range(PROJ_PARTS):
        r0 = part * rows
        xp = x[r0:r0 + rows]
        y = xp * lax.rsqrt(jnp.mean(xp * xp, axis=-1, keepdims=True) + EPS) * g_ref[0]
        y3 = y.reshape(tt, BATCH, d)
        n = (y3 * (1.0 + scale)[None] + shift[None]).reshape(rows, d).astype(BF16)
        pf_ref[r0:r0 + rows, :] = jnp.dot(n, wf_ref[0], preferred_element_type=F32)
        att = jnp.dot(n, wa_ref[0], preferred_element_type=F32)
        cos = jnp.where(use_col, csr_ref[r0:r0 + rows, :], csp_ref[pl.ds(grid_row, 1), :])
        sin = jnp.where(use_col, snr_ref[r0:r0 + rows, :], snp_ref[pl.ds(grid_row, 1), :])
        cos = jnp.where(is_ctx, 1.0, cos)
        sin = jnp.where(is_ctx, 0.0, sin)
        k = _rope(att[:, :W_DA], cos, sin)
        v = att[:, W_DA:2 * W_DA]
        q = _rope(att[:, 2 * W_DA:], cos, sin) * (DA_DH ** -0.5 * math.log2(math.e))
        _rows_to_batch(k, scr, 0, kq_ref, 0, BF16, part * tt)
        _rows_to_batch(q, scr, slabs, kq_ref, W_DA, BF16, part * tt)
        _rows_to_batch(v, scr, 2 * slabs, v_ref, 0, BF16, part * tt)


def _in_proj_kernel(ctx_ref, x_ref, *refs, ncb):
    proj_in, (hs_ref, pf_ref, kq_ref, v_ref, scr, rows_scr) = refs[:8], refs[8:]
    i = pl.program_id(0)

    @pl.when(i < ncb)
    def _():
        hs_ref[...] = _batch_to_rows(ctx_ref, rows_scr)

    @pl.when(i >= ncb)
    def _():
        hs_ref[...] = _batch_to_rows(x_ref, rows_scr)

    _project(i, hs_ref[...], *proj_in, pf_ref, kq_ref, v_ref, scr, ncb)


def _project_specs(layer, proj, tt, n_rows, kind):
    mods, g, wf, wa, tables = proj
    d = wf.shape[1]
    tm = tt * BATCH
    s_len = n_rows // BATCH
    const = lambda shape: pl.BlockSpec(shape, lambda i: (0,) * len(shape),
                                       pipeline_mode=pl.Buffered(1))
    in_specs = [pl.BlockSpec((1, 1, 3, BATCH, d), lambda i: (layer, kind(i), 0, 0, 0)),
                _layer_block(g, layer), _layer_block(wf, layer), _layer_block(wa, layer)
                ] + [const(t.shape) for t in tables]
    out_specs = [pl.BlockSpec((tm, PF_W), lambda i: (i, 0)),
                 pl.BlockSpec((BATCH, tt, 2 * W_DA), lambda i: (0, i, 0)),
                 pl.BlockSpec((BATCH, tt, W_DA), lambda i: (0, i, 0))]
    out_shape = [jax.ShapeDtypeStruct((n_rows, PF_W), F32),
                 jax.ShapeDtypeStruct((BATCH, s_len, 2 * W_DA), BF16),
                 jax.ShapeDtypeStruct((BATCH, s_len, W_DA), BF16)]
    scratch = pltpu.VMEM((PA_W // LANES, tm, LANES), F32)
    return in_specs, out_specs, out_shape, scratch


def _in_proj(ctx, x, proj):
    bsz, n_ctx, d = ctx.shape
    tt = TOKEN_TILE
    assert tt == GRID_W
    tm = tt * BATCH
    ncb = n_ctx // tt
    n = ncb + x.shape[1] // tt
    kind = lambda i: (i >= ncb).astype(jnp.int32)
    in_specs, out_specs, out_shape, scratch = _project_specs(0, proj, tt, n * tm, kind)
    return pl.pallas_call(
        functools.partial(_in_proj_kernel, ncb=ncb),
        grid=(n,),
        in_specs=[pl.BlockSpec((bsz, tt, d), lambda i: (0, jnp.minimum(i, ncb - 1), 0)),
                  pl.BlockSpec((bsz, tt, d), lambda i: (0, jnp.maximum(i - ncb, 0), 0))] + in_specs,
        out_specs=[pl.BlockSpec((tm, d), lambda i: (i, 0))] + out_specs,
        out_shape=[jax.ShapeDtypeStruct((n * tm, d), F32)] + out_shape,
        scratch_shapes=[scratch, pltpu.VMEM((d // LANES, tm, LANES), F32)],
        compiler_params=_cparams(("arbitrary",)),
    )(ctx, x, *proj[:4], *proj[4])


def _bwd_chunk(i, ncc, n):
    return jnp.where(i < ncc, ncc - 1 - i, n - 1 - i + ncc)


S5_LANE_CHUNK = 512
N_LRU_IN, N_S5_IN = 11, 9


def _lru_gates(d, c, u_c, u_p, u_n, cw_ref, cb_ref, wg_ref, bg_ref, lam_ref, a_scr, b_scr, ts, ncc, n):
    seg_start = jnp.logical_or(c == 0, c == ncc)
    seg_end = jnp.logical_or(c == ncc - 1, c == n - 1)
    prev = u_p[...] * jnp.where(seg_start, 0.0, 1.0)
    nxt = u_n[...] * jnp.where(seg_end, 0.0, 1.0)
    u = jnp.concatenate([prev, u_c[...], nxt], axis=0)
    cw = cw_ref[0]
    xc = cb_ref[0] + cw[0:1] * u[0:ts]
    for j in range(1, CONV_W):
        xc = xc + cw[j:j + 1] * u[j:j + ts]
    xc2 = xc.reshape(ts * xc.shape[1], W_LRU)
    gts = jnp.dot(xc2.astype(BF16), wg_ref[0, d], preferred_element_type=F32) + bg_ref[0, d]
    gate_r = jax.nn.sigmoid(gts[:, :W_LRU])
    gate_i = jax.nn.sigmoid(gts[:, W_LRU:])
    log_a = -LRU_C * gate_r * _softplus(-lam_ref[0, d])
    a = jnp.exp(log_a)
    th = jnp.tanh(log_a)
    one_minus_a2 = -2.0 * th / (1.0 - th)
    bb = jnp.sqrt(one_minus_a2) * (gate_i * xc2)
    a_scr[d] = a.reshape(xc.shape)
    b_scr[d] = bb.reshape(xc.shape)


def _s5_discretise(lre_ref, lim_ref, ldt_ref, bre_ref, bim_ref, bbar_scr, a_scr):
    bsz, nst = a_scr.shape[2:]
    for d in range(2):
        lre = lre_ref[0, d]
        lim = lim_ref[0, d]
        dt = jnp.exp(ldt_ref[0, d])
        mag = jnp.exp(lre * dt)
        ang = lim * dt
        ar = mag * jnp.cos(ang)
        ai = mag * jnp.sin(ang)
        nr = ar - 1.0
        den = lre * lre + lim * lim
        cfr = (nr * lre + ai * lim) / den
        cfi = (ai * lre - nr * lim) / den
        bre = bre_ref[0, d]
        bim = bim_ref[0, d]
        bbar_scr[d, :, :nst] = (cfr * bre - cfi * bim).astype(BF16)
        bbar_scr[d, :, nst:] = (cfr * bim + cfi * bre).astype(BF16)
        a_scr[d, 0] = jnp.broadcast_to(ar, (bsz, nst))
        a_scr[d, 1] = jnp.broadcast_to(ai, (bsz, nst))


def _scan_kernel(*refs, ts, ncc, n):
    lru_in, s5_in = refs[:N_LRU_IN], refs[N_LRU_IN:N_LRU_IN + N_S5_IN]
    hf_ref, hb_ref, yf_ref, yb_ref = refs[N_LRU_IN + N_S5_IN:N_LRU_IN + N_S5_IN + 4]
    la_scr, lb_scr, lcarry_scr, bbar_scr, sa_scr, buf_scr, scarry_scr = refs[N_LRU_IN + N_S5_IN + 4:]
    ufc, ufp, ufn, ubc, ubp, ubn, cw_ref, cb_ref, wg_ref, bg_ref, lam_ref = lru_in
    uf_ref, ub_ref, lre_ref, lim_ref, ldt_ref, bre_ref, bim_ref, cre_ref, cim_ref = s5_in
    i = pl.program_id(0)
    nst = S5_N
    bsz = uf_ref.shape[1]

    @pl.when(i == 0)
    def _():
        lcarry_scr[...] = jnp.zeros_like(lcarry_scr)
        scarry_scr[...] = jnp.zeros_like(scarry_scr)
        _s5_discretise(lre_ref, lim_ref, ldt_ref, bre_ref, bim_ref, bbar_scr, sa_scr)

    for d, u_ref in enumerate((uf_ref, ub_ref)):
        u2 = u_ref[...].reshape(ts * bsz, W_S5).astype(BF16)
        bu = jnp.dot(u2, bbar_scr[d], preferred_element_type=F32)
        buf_scr[d] = bu.reshape(ts, bsz, 2 * nst)

    lru_par = (cw_ref, cb_ref, wg_ref, bg_ref, lam_ref, la_scr, lb_scr, ts, ncc, n)
    _lru_gates(0, i, ufc, ufp, ufn, *lru_par)
    _lru_gates(1, _bwd_chunk(i, ncc, n), ubc, ubp, ubn, *lru_par)
    hf = lcarry_scr[0]
    hb = lcarry_scr[1]
    for k in range(ts):
        kb = ts - 1 - k
        hf = la_scr[0, k] * hf + lb_scr[0, k]
        hb = la_scr[1, kb] * hb + lb_scr[1, kb]
        hf_ref[k] = hf
        hb_ref[kb] = hb
    lcarry_scr[0] = hf
    lcarry_scr[1] = hb

    lc = S5_LANE_CHUNK
    chains = [(d, pl.ds(c0, lc), pl.ds(nst + c0, lc)) for d in range(2) for c0 in range(0, nst, lc)]
    state = [(scarry_scr[d, :, re_sl], scarry_scr[d, :, im_sl]) for d, re_sl, im_sl in chains]
    for k in range(ts):
        for n_chain, (d, re_sl, im_sl) in enumerate(chains):
            t = k if d == 0 else ts - 1 - k
            ar = sa_scr[d, 0, :, re_sl]
            ai = sa_scr[d, 1, :, re_sl]
            sre, sim = state[n_chain]
            nre = ar * sre - ai * sim + buf_scr[d, t, :, re_sl]
            nim = ar * sim + ai * sre + buf_scr[d, t, :, im_sl]
            buf_scr[d, t, :, re_sl] = nre
            buf_scr[d, t, :, im_sl] = nim
            state[n_chain] = (nre, nim)
    for (d, re_sl, im_sl), (sre, sim) in zip(chains, state):
        scarry_scr[d, :, re_sl] = sre
        scarry_scr[d, :, im_sl] = sim

    for d, y_ref in enumerate((yf_ref, yb_ref)):
        st = buf_scr[d].reshape(ts * bsz, 2 * nst)
        y = (jnp.dot(st[:, :nst].astype(BF16), cre_ref[0, d], preferred_element_type=F32)
             - jnp.dot(st[:, nst:].astype(BF16), cim_ref[0, d], preferred_element_type=F32))
        y_ref[...] = y.reshape(ts, bsz, W_S5)


def _scans(pf3, layer, lru_params, s5_params, *, n_ctx):
    s_len, bsz, _ = pf3.shape
    ts = SCAN_TILE
    n = s_len // ts
    ncc = n_ctx // ts
    fwd = lambda i: i
    bwd = lambda i: _bwd_chunk(i, ncc, n)
    s5_col = W_LRU // W_S5

    def cur(cfn, col=0):
        return pl.BlockSpec((ts, bsz, W_LRU), lambda i: (cfn(i), 0, col))

    def prev(cfn):
        return pl.BlockSpec((2, bsz, W_LRU), lambda i: (jnp.maximum(cfn(i) * (ts // 2) - 1, 0), 0, 0))

    def nxt(cfn):
        return pl.BlockSpec((1, bsz, W_LRU), lambda i: (jnp.minimum((cfn(i) + 1) * ts, s_len - 1), 0, 0))

    const = lambda a: _layer_block(a, layer)
    assert len(lru_params) == N_LRU_IN - 6 and len(s5_params) == N_S5_IN - 2
    out_block = lambda cfn: pl.BlockSpec((ts, bsz, W_LRU), lambda i: (cfn(i), 0, 0))
    return pl.pallas_call(
        functools.partial(_scan_kernel, ts=ts, ncc=ncc, n=n),
        grid=(n,),
        in_specs=([cur(fwd), prev(fwd), nxt(fwd), cur(bwd), prev(bwd), nxt(bwd)]
                  + [const(a) for a in lru_params]
                  + [cur(fwd, s5_col), cur(bwd, s5_col)] + [const(a) for a in s5_params]),
        out_specs=[out_block(fwd), out_block(bwd), out_block(fwd), out_block(bwd)],
        out_shape=[jax.ShapeDtypeStruct((s_len, bsz, W_LRU), F32)] * 4,
        scratch_shapes=[pltpu.VMEM((2, ts, bsz, W_LRU), F32),
                        pltpu.VMEM((2, ts, bsz, W_LRU), F32),
                        pltpu.VMEM((2, bsz, W_LRU), F32),
                        pltpu.VMEM((2, W_S5, 2 * S5_N), BF16),
                        pltpu.VMEM((2, 2, bsz, S5_N), F32),
                        pltpu.VMEM((2, ts, bsz, 2 * S5_N), F32),
                        pltpu.VMEM((2, bsz, 2 * S5_N), F32)],
        compiler_params=_cparams(("arbitrary",)),
    )(*([pf3] * 6), *lru_params, pf3, pf3, *s5_params)


V_ROWS = DA_DV + 16
SCORE_PARTS = 2
SCORE_SLOTS = 4


def _attn_kernel(q_ref, k_ref, v_ref, dl_ref, g_ref, o_ref, s_scr, vt_scr, *, tq, n_ctx, lam_init):
    s_len = k_ref.shape[1]
    nq = s_len // tq
    ncq = n_ctx // tq
    for c0 in range(0, s_len, tq):
        vt_scr[0:DA_DV, c0:c0 + tq] = v_ref[0, c0:c0 + tq, :].astype(F32).T.astype(BF16)
    pad_row = lax.broadcasted_iota(jnp.int32, (V_ROWS - DA_DV, s_len), 0)
    vt_scr[DA_DV:, :] = jnp.where(pad_row == 0, 1.0, 0.0).astype(BF16)
    dl = dl_ref[0]
    lam = (jnp.exp(jnp.sum(dl[0:1] * dl[1:2], axis=-1, keepdims=True))
           - jnp.exp(jnp.sum(dl[2:3] * dl[3:4], axis=-1, keepdims=True)) + lam_init)
    gscale = g_ref[0] * (1.0 - lam_init)
    lane = lax.broadcasted_iota(jnp.int32, (tq, LANES), 1)

    def rows(qt):
        return pl.ds(pl.multiple_of(qt * tq, tq), tq)

    def key_parts(nk, max_parts):
        groups = nk // LANES
        n_parts = min(max_parts, groups)
        bounds = [LANES * ((groups * p) // n_parts) for p in range(n_parts + 1)]
        return tuple(zip(bounds[:-1], bounds[1:]))

    def scores(qt, mp, nk, slot):
        q = q_ref[0, rows(qt), :]
        qm = jnp.where((lane < DA_DH) if mp == 0 else (lane >= DA_DH), q, jnp.zeros_like(q))
        m = None
        for lo, hi in key_parts(nk, SCORE_PARTS):
            s_t = lax.dot_general(k_ref[0, lo:hi, :], qm, (((1,), (1,)), ((), ())),
                                  preferred_element_type=F32)
            s_scr[slot, lo:hi, :] = s_t
            m_part = jnp.max(s_t, axis=0, keepdims=True)
            m = m_part if m is None else jnp.maximum(m, m_part)
        return m

    def values(slot, m, nk):
        p_t = jnp.exp2(s_scr[slot, 0:nk, :] - m).astype(BF16)
        return jnp.dot(vt_scr[:, 0:nk], p_t, preferred_element_type=F32)

    def finish(qt, o1, o2):
        o_t = (o1[:DA_DV] * (1.0 / o1[DA_DV:DA_DV + 1])
               - o2[:DA_DV] * (lam / o2[DA_DV:DA_DV + 1]))
        y_t = o_t * lax.rsqrt(jnp.mean(o_t * o_t, axis=0, keepdims=True) + EPS)
        o_ref[0, rows(qt), :] = y_t.T * gscale

    for qt in range(ncq):
        m0 = scores(qt, 0, n_ctx, 0)
        m1 = scores(qt, 1, n_ctx, 2)
        finish(qt, values(0, m0, n_ctx), values(2, m1, n_ctx))

    def pair(t0, m0, last):
        m1 = scores(t0, 1, s_len, 2)
        o1 = values(0, m0, s_len)
        m0b = scores(t0 + 1, 0, s_len, 1)
        o2 = values(2, m1, s_len)
        finish(t0, o1, o2)
        m1b = scores(t0 + 1, 1, s_len, 3)
        o1b = values(1, m0b, s_len)
        m0_next = None if last else scores(t0 + 2, 0, s_len, 0)
        o2b = values(3, m1b, s_len)
        finish(t0 + 1, o1b, o2b)
        return m0_next

    n_pairs = (nq - ncq) // 2
    m0 = lax.fori_loop(0, n_pairs - 1, lambda j, m0: pair(ncq + 2 * j, m0, False),
                       scores(ncq, 0, s_len, 0))
    pair(nq - 2, m0, True)


def _attention(qk, v, layer, da_lam, da_g, *, n_ctx, lam_init):
    bsz, s_len, _ = qk.shape
    tq = Q_TILE
    assert n_ctx % tq == 0 and (s_len - n_ctx) % (2 * tq) == 0
    return pl.pallas_call(
        functools.partial(_attn_kernel, tq=tq, n_ctx=n_ctx, lam_init=lam_init),
        grid=(bsz, DA_HEADS),
        in_specs=[pl.BlockSpec((1, s_len, LANES), lambda b, h: (b, 0, DA_HEADS + h)),
                  pl.BlockSpec((1, s_len, LANES), lambda b, h: (b, 0, h)),
                  pl.BlockSpec((1, s_len, LANES), lambda b, h: (b, 0, h)),
                  _layer_block(da_lam, layer), _layer_block(da_g, layer)],
        out_specs=pl.BlockSpec((1, s_len, LANES), lambda b, h: (b, 0, h)),
        out_shape=jax.ShapeDtypeStruct((bsz, s_len, W_DA), F32),
        scratch_shapes=[pltpu.VMEM((SCORE_SLOTS, s_len, tq), F32),
                        pltpu.VMEM((V_ROWS, s_len), BF16)],
        compiler_params=_cparams(("arbitrary", "arbitrary")),
    )(qk, qk, v, da_lam, da_g)


def _gelu_tanh(x):
    return 0.5 * x * (1.0 + jnp.tanh(math.sqrt(2.0 / math.pi) * (x + 0.044715 * (x * x * x))))


N_MIX_REFS = 15


def _out_proj_kernel(*refs, final, ncb):
    (hf_ref, hb_ref, yf_ref, yb_ref, us_ref, ga_ref, gs_ref, gd_ref, yd_ref, h_ref, mod_ref,
     d_ref, wglu_ref, bglu_ref, wo_ref) = refs[:N_MIX_REFS]
    yd_scr = refs[-1]
    y_a = (hf_ref[...] + hb_ref[...]) * _silu(ga_ref[...])
    z = _gelu_tanh(d_ref[0] * us_ref[...] + yf_ref[...] + yb_ref[...])
    glu = jax.nn.sigmoid(jnp.dot(z.astype(BF16), wglu_ref[0], preferred_element_type=F32)
                         + bglu_ref[0])
    y_s = z * glu * _silu(gs_ref[...])
    y_d = _batch_to_rows(yd_ref, yd_scr) * _silu(gd_ref[...])
    out = (jnp.dot(y_a.astype(BF16), wo_ref[0, 0:W_LRU, :], preferred_element_type=F32)
           + jnp.dot(y_s.astype(BF16), wo_ref[0, W_LRU:W_LRU + W_S5, :], preferred_element_type=F32)
           + jnp.dot(y_d.astype(BF16), wo_ref[0, W_LRU + W_S5:, :], preferred_element_type=F32))
    rows, d = out.shape
    gate = mod_ref[0, 0, 2]
    hn = h_ref[...] + (out.reshape(rows // BATCH, BATCH, d) * gate[None]).reshape(rows, d)
    if final:
        fg_ref, o_ref, out_scr = refs[N_MIX_REFS:-1]
        hn = hn * lax.rsqrt(jnp.mean(hn * hn, axis=-1, keepdims=True) + EPS) * fg_ref[...]
        _rows_to_batch(hn, out_scr, 0, o_ref, 0, F32)
    else:
        n_proj_in = 8
        hs_ref = refs[N_MIX_REFS + n_proj_in]
        hs_ref[...] = hn
        _project(pl.program_id(0), hn, *refs[N_MIX_REFS:N_MIX_REFS + n_proj_in],
                 *refs[N_MIX_REFS + n_proj_in + 1:-1], ncb)


def _out_proj(hf, hb, yf, yb, pf, yd, hs, layer, mods, s5_d, w_glu, b_glu, w_out, *, n_ctx,
              final_g=None, next_proj=None):
    final = final_g is not None
    n_rows, d = hs.shape
    tt = TOKEN_TILE
    tm = tt * BATCH
    ncb = n_ctx // tt
    off = ncb if final else 0
    nblk = n_rows // tm - off
    blk = lambda w, c=0: pl.BlockSpec((tm, w), lambda i: (i + off, c))
    const = lambda a: _layer_block(a, layer)
    kind = lambda i: (i + off >= ncb).astype(jnp.int32)
    in_specs = [blk(W_LRU), blk(W_LRU), blk(W_S5), blk(W_S5),
                blk(W_S5, 1), blk(W_LRU, 2), blk(W_S5, 3), blk(W_DA, 2),
                pl.BlockSpec((BATCH, tt, W_DA), lambda i: (0, i + off, 0)), blk(d),
                pl.BlockSpec((1, 1, 3, BATCH, d), lambda i: (layer, kind(i), 0, 0, 0)),
                const(s5_d), const(w_glu), const(b_glu), const(w_out)]
    operands = [hf, hb, yf, yb, pf, pf, pf, pf, yd, hs, mods, s5_d, w_glu, b_glu, w_out]
    assert len(operands) == N_MIX_REFS
    yd_scratch = pltpu.VMEM((W_DA // LANES, tm, LANES), F32)
    if final:
        final_g = final_g.reshape(1, d)
        in_specs += [pl.BlockSpec(final_g.shape, lambda i: (0, 0))]
        operands += [final_g]
        out_specs = pl.BlockSpec((BATCH, tt, d), lambda i: (0, i, 0))
        out_shape = jax.ShapeDtypeStruct((BATCH, nblk * tt, d), F32)
        scratch = [pltpu.VMEM((d // LANES, tm, LANES), F32), yd_scratch]
    else:
        p_in, p_out, p_shape, p_scratch = _project_specs(layer + 1, next_proj, tt, n_rows, kind)
        in_specs += p_in
        operands += [*next_proj[:4], *next_proj[4]]
        out_specs = [pl.BlockSpec((tm, d), lambda i: (i, 0))] + p_out
        out_shape = [jax.ShapeDtypeStruct((n_rows, d), F32)] + p_shape
        scratch = [p_scratch, yd_scratch]
    return pl.pallas_call(
        functools.partial(_out_proj_kernel, final=final, ncb=ncb),
        grid=(nblk,),
        in_specs=in_specs,
        out_specs=out_specs,
        out_shape=out_shape,
        scratch_shapes=scratch,
        compiler_params=_cparams(("arbitrary",)),
    )(*operands)


def _block_diag(w):
    m, nb, k, n = w.shape
    eye = jnp.eye(nb, dtype=w.dtype)
    return jnp.einsum('bnkd,nm->bnkmd', w, eye).reshape(m, nb * k, nb * n)


def kernel(x, c, ctx, c_ctx, norm_g, w_mod, b_mod, w_in, w_out, lru_conv_w, lru_conv_b, lru_wa, lru_ba,
           lru_wx, lru_bx, lru_lam, s5_lam_re, s5_lam_im, s5_log_dt, s5_b_re, s5_b_im, s5_c_re, s5_c_im,
           s5_d, s5_w_glu, s5_b_glu, da_lam, da_norm_g, final_g):
    bsz, t_len, d = x.shape
    n_ctx = ctx.shape[1]
    s_len = n_ctx + t_len
    n_rows = s_len * bsz
    n_layers = w_in.shape[0]
    assert bsz == BATCH and bsz + 1 <= MOD_ROWS
    assert n_ctx % Q_TILE == 0 and t_len % Q_TILE == 0
    assert n_ctx % TOKEN_TILE == 0 and n_ctx % SCAN_TILE == 0

    cc = jnp.concatenate([c, c_ctx[None], jnp.zeros((MOD_ROWS - bsz - 1, d), F32)], axis=0)
    mod = _modulation(cc, w_mod.astype(BF16), b_mod).reshape(n_layers, MOD_ROWS, 3, d)
    mod_lat = mod[:, :bsz].transpose(0, 2, 1, 3)
    mod_ctx = jnp.broadcast_to(mod[:, bsz:bsz + 1], (n_layers, bsz, 3, d)).transpose(0, 2, 1, 3)
    mods = jnp.stack([mod_ctx, mod_lat], axis=1)

    tables = _rope_tables()

    side = W_LRU + W_S5 + 2 * W_DA
    c_ua, c_us = (0, W_LRU), (W_LRU, W_LRU + W_S5)
    c_k, c_v = (W_LRU + W_S5, W_LRU + W_S5 + W_DA), (W_LRU + W_S5 + W_DA, side)
    sl = lambda lo_hi, base=0: w_in[:, :, base + lo_hi[0]:base + lo_hi[1]]
    wf = jnp.concatenate([sl(c_ua), sl(c_us), sl(c_ua, side), sl(c_us, side), sl(c_v, side)],
                         axis=-1).astype(BF16)
    wa = jnp.concatenate([sl(c_k), sl(c_v), sl(c_k, side)], axis=-1).astype(BF16)
    w_out_b = w_out.astype(BF16)
    w_glu_b = s5_w_glu.astype(BF16)

    bd = lambda w: _block_diag(w.reshape((-1,) + w.shape[-3:])).reshape(
        w.shape[:-3] + (w.shape[-3] * w.shape[-2], w.shape[-3] * w.shape[-1]))
    wg = jnp.concatenate([bd(lru_wa), bd(lru_wx)], axis=-1).astype(BF16)
    bg = jnp.concatenate([lru_ba, lru_bx], axis=-1).reshape(n_layers, 2, 1, 2 * W_LRU)
    lru_lam_r = lru_lam.reshape(n_layers, 2, 1, W_LRU)
    lru_conv_b_r = lru_conv_b.reshape(n_layers, 1, W_LRU)
    flat = lambda a: a.reshape(n_layers, 2, 1, S5_N)
    ldt = jnp.broadcast_to(s5_log_dt[..., None], (n_layers, 2, S5_G, S5_P))
    bre = bd(jnp.swapaxes(s5_b_re, -1, -2))
    bim = bd(jnp.swapaxes(s5_b_im, -1, -2))
    cre = bd(jnp.swapaxes(s5_c_re, -1, -2)).astype(BF16)
    cim = bd(jnp.swapaxes(s5_c_im, -1, -2)).astype(BF16)
    lam_re, lam_im, ldt = flat(s5_lam_re), flat(s5_lam_im), flat(ldt)

    lru_params = (lru_conv_w, lru_conv_b_r, wg, bg, lru_lam_r)
    s5_params = (lam_re, lam_im, ldt, bre, bim, cre, cim)
    proj = (mods, norm_g.reshape(n_layers, 1, d), wf, wa, tables)
    da_lam_r = da_lam.reshape(n_layers, 4, DA_DH)
    da_g_r = da_norm_g.reshape(n_layers, 1, DA_DV)
    s5_d_r = s5_d.reshape(n_layers, 1, W_S5)
    b_glu_r = s5_b_glu.reshape(n_layers, 1, W_S5)

    hs, pf, kq, v = _in_proj(ctx, x, proj)
    for l in range(n_layers):
        final = l == n_layers - 1
        lam_init = 0.8 - 0.6 * math.exp(-0.3 * l)
        pf3 = pf.reshape(s_len, bsz, PF_W)

        hf, hb, yf, yb = _scans(pf3, l, lru_params, s5_params, n_ctx=n_ctx)

        yd = _attention(kq, v, l, da_lam_r, da_g_r, n_ctx=n_ctx, lam_init=lam_init)

        mix = (hf.reshape(n_rows, W_LRU), hb.reshape(n_rows, W_LRU),
               yf.reshape(n_rows, W_S5), yb.reshape(n_rows, W_S5),
               pf, yd, hs, l, mods, s5_d_r, w_glu_b, b_glu_r, w_out_b)
        if final:
            return _out_proj(*mix, n_ctx=n_ctx, final_g=final_g)
        hs, pf, kq, v = _out_proj(*mix, n_ctx=n_ctx, next_proj=proj)
```

```python
import functools
import math

import jax
import jax.numpy as jnp
from jax import lax
from jax.experimental import pallas as pl
from jax.experimental.pallas import tpu as pltpu

F32 = jnp.float32
BF16 = jnp.bfloat16

GRID_W = 64
EPS = 1e-6
W_LRU = 256
LRU_BLOCKS = 4
CONV_W = 4
LRU_C = 8.0
W_S5 = 256
S5_H = 16
S5_G = 16
S5_P = 64
S5_N = S5_G * S5_P
DA_HEADS = 4
DA_DH = 64
DA_DV = 2 * DA_DH
W_DA = DA_HEADS * DA_DV
ROPE_BASE = 10000.0
ROPE_F = DA_DH // 4
LANES = 128
BATCH = 8

PF_W = 2 * W_LRU + 2 * W_S5 + W_DA
PA_W = 3 * W_DA

TOKEN_TILE = 64
PROJ_PARTS = 2
SCAN_TILE = 64
Q_TILE = 256
MOD_ROWS = 16
VMEM_LIMIT = 48 * 1024 * 1024


def _layer_block(a, layer):
    index = (layer,) + (0,) * (a.ndim - 1)
    return pl.BlockSpec((1,) + a.shape[1:], lambda *_: index, pipeline_mode=pl.Buffered(1))


def _cparams(sem):
    return pltpu.CompilerParams(dimension_semantics=sem, vmem_limit_bytes=VMEM_LIMIT)


def _silu(x):
    return x * jax.nn.sigmoid(x)


def _softplus(x):
    return jnp.maximum(x, 0.0) + jnp.log1p(jnp.exp(-jnp.abs(x)))


def _rows_to_batch(val, scr, slab0, out_ref, col0, dtype, t0=0):
    rows, width = val.shape
    tt = rows // BATCH
    r0 = t0 * BATCH
    for j in range(width // LANES):
        scr[slab0 + j, r0:r0 + rows, :] = val[:, j * LANES:(j + 1) * LANES]
    for b in range(BATCH):
        for j in range(width // LANES):
            out_ref[b, t0:t0 + tt, col0 + j * LANES:col0 + (j + 1) * LANES] = (
                scr[slab0 + j, pl.ds(r0 + b, tt, stride=BATCH), :].astype(dtype))


def _batch_to_rows(in_ref, scr):
    _, tt, width = in_ref.shape
    for b in range(BATCH):
        for j in range(width // LANES):
            scr[j, pl.ds(b, tt, stride=BATCH), :] = in_ref[b, :, j * LANES:(j + 1) * LANES].astype(F32)
    return jnp.concatenate([scr[j] for j in range(width // LANES)], axis=1)


def _rope_table_kernel(csp_ref, snp_ref, csr_ref, snr_ref):
    def table(shape, pos_shift):
        pos = lax.broadcasted_iota(jnp.int32, shape, 0) >> pos_shift
        lane = lax.broadcasted_iota(jnp.int32, shape, 1)
        f = (lane & (ROPE_F - 1)).astype(F32)
        ang = pos.astype(F32) * jnp.exp(-math.log(ROPE_BASE) * f / ROPE_F)
        sn = jnp.sin(ang)
        return jnp.cos(ang), jnp.where((lane & ROPE_F) == 0, -sn, sn)

    csp_ref[...], snp_ref[...] = table(csp_ref.shape, 0)
    csr_ref[...], snr_ref[...] = table(csr_ref.shape, int(math.log2(BATCH)))


def _rope_tables():
    shapes = [(GRID_W, LANES)] * 2 + [(GRID_W * BATCH, LANES)] * 2
    return pl.pallas_call(
        _rope_table_kernel,
        out_shape=[jax.ShapeDtypeStruct(s, F32) for s in shapes],
        compiler_params=pltpu.CompilerParams(vmem_limit_bytes=VMEM_LIMIT),
    )()


def _mod_kernel(c_ref, w_ref, b_ref, o_ref):
    c = c_ref[...]
    o_ref[0] = (jnp.dot(_silu(c).astype(BF16), w_ref[0].astype(BF16), preferred_element_type=F32)
                + b_ref[0])


def _modulation(cc, w_mod, b_mod):
    n_layers, d, d3 = w_mod.shape
    return pl.pallas_call(
        _mod_kernel,
        grid=(n_layers, d3 // d),
        in_specs=[pl.BlockSpec((MOD_ROWS, d), lambda l, j: (0, 0)),
                  pl.BlockSpec((1, d, d), lambda l, j: (l, 0, j)),
                  pl.BlockSpec((1, 1, d), lambda l, j: (l, 0, j))],
        out_specs=pl.BlockSpec((1, MOD_ROWS, d), lambda l, j: (l, 0, j)),
        out_shape=jax.ShapeDtypeStruct((n_layers, MOD_ROWS, d3), F32),
        compiler_params=_cparams(("arbitrary", "arbitrary")),
    )(cc, w_mod, b_mod.reshape(n_layers, 1, d3))


def _rope(x, cos, sin):
    width = x.shape[1]
    reps = width // LANES
    cos = jnp.concatenate([cos] * reps, axis=1)
    sin = jnp.concatenate([sin] * reps, axis=1)
    lane = lax.broadcasted_iota(jnp.int32, x.shape, 1)
    first = (lane & ROPE_F) == 0
    partner = jnp.where(first, pltpu.roll(x, width - ROPE_F, 1), pltpu.roll(x, ROPE_F, 1))
    return x * cos + partner * sin


def _project(i, x, mod_ref, g_ref, wf_ref, wa_ref, csp_ref, snp_ref, csr_ref, snr_ref,
             pf_ref, kq_ref, v_ref, scr, ncb):
    rows_all, d = x.shape
    rows = rows_all // PROJ_PARTS
    tt = rows // BATCH
    shift = mod_ref[0, 0, 0]
    scale = mod_ref[0, 0, 1]
    grid_row = jnp.maximum(i - ncb, 0)
    is_ctx = i < ncb
    lane = lax.broadcasted_iota(jnp.int32, (rows, LANES), 1)
    use_col = (lane & (DA_DH // 2)) != 0
    slabs = W_DA // LANES
    for part in range(PROJ_PARTS):
        r0 = part * rows
        xp = x[r0:r0 + rows]
        y = xp * lax.rsqrt(jnp.mean(xp * xp, axis=-1, keepdims=True) + EPS) * g_ref[0]
        y3 = y.reshape(tt, BATCH, d)
        n = (y3 * (1.0 + scale)[None] + shift[None]).reshape(rows, d).astype(BF16)
        pf_ref[r0:r0 + rows, :] = jnp.dot(n, wf_ref[0], preferred_element_type=F32)
        att = jnp.dot(n, wa_ref[0], preferred_element_type=F32)
        cos = jnp.where(use_col, csr_ref[r0:r0 + rows, :], csp_ref[pl.ds(grid_row, 1), :])
        sin = jnp.where(use_col, snr_ref[r0:r0 + rows, :], snp_ref[pl.ds(grid_row, 1), :])
        cos = jnp.where(is_ctx, 1.0, cos)
        sin = jnp.where(is_ctx, 0.0, sin)
        k = _rope(att[:, :W_DA], cos, sin)
        v = att[:, W_DA:2 * W_DA]
        q = _rope(att[:, 2 * W_DA:], cos, sin) * (DA_DH ** -0.5 * math.log2(math.e))
        _rows_to_batch(k, scr, 0, kq_ref, 0, BF16, part * tt)
        _rows_to_batch(q, scr, slabs, kq_ref, W_DA, BF16, part * tt)
        _rows_to_batch(v, scr, 2 * slabs, v_ref, 0, BF16, part * tt)


def _in_proj_kernel(ctx_ref, x_ref, *refs, ncb):
    proj_in, (hs_ref, pf_ref, kq_ref, v_ref, scr, rows_scr) = refs[:8], refs[8:]
    i = pl.program_id(0)

    @pl.when(i < ncb)
    def _():
        hs_ref[...] = _batch_to_rows(ctx_ref, rows_scr)

    @pl.when(i >= ncb)
    def _():
        hs_ref[...] = _batch_to_rows(x_ref, rows_scr)

    _project(i, hs_ref[...], *proj_in, pf_ref, kq_ref, v_ref, scr, ncb)


def _project_specs(layer, proj, tt, n_rows, kind):
    mods, g, wf, wa, tables = proj
    d = wf.shape[1]
    tm = tt * BATCH
    s_len = n_rows // BATCH
    const = lambda shape: pl.BlockSpec(shape, lambda i: (0,) * len(shape),
                                       pipeline_mode=pl.Buffered(1))
    in_specs = [pl.BlockSpec((1, 1, 3, BATCH, d), lambda i: (layer, kind(i), 0, 0, 0)),
                _layer_block(g, layer), _layer_block(wf, layer), _layer_block(wa, layer)
                ] + [const(t.shape) for t in tables]
    out_specs = [pl.BlockSpec((tm, PF_W), lambda i: (i, 0)),
                 pl.BlockSpec((BATCH, tt, 2 * W_DA), lambda i: (0, i, 0)),
                 pl.BlockSpec((BATCH, tt, W_DA), lambda i: (0, i, 0))]
    out_shape = [jax.ShapeDtypeStruct((n_rows, PF_W), F32),
                 jax.ShapeDtypeStruct((BATCH, s_len, 2 * W_DA), BF16),
                 jax.ShapeDtypeStruct((BATCH, s_len, W_DA), BF16)]
    scratch = pltpu.VMEM((PA_W // LANES, tm, LANES), F32)
    return in_specs, out_specs, out_shape, scratch


def _in_proj(ctx, x, proj):
    bsz, n_ctx, d = ctx.shape
    tt = TOKEN_TILE
    assert tt == GRID_W
    tm = tt * BATCH
    ncb = n_ctx // tt
    n = ncb + x.shape[1] // tt
    kind = lambda i: (i >= ncb).astype(jnp.int32)
    in_specs, out_specs, out_shape, scratch = _project_specs(0, proj, tt, n * tm, kind)
    return pl.pallas_call(
        functools.partial(_in_proj_kernel, ncb=ncb),
        grid=(n,),
        in_specs=[pl.BlockSpec((bsz, tt, d), lambda i: (0, jnp.minimum(i, ncb - 1), 0)),
                  pl.BlockSpec((bsz, tt, d), lambda i: (0, jnp.maximum(i - ncb, 0), 0))] + in_specs,
        out_specs=[pl.BlockSpec((tm, d), lambda i: (i, 0))] + out_specs,
        out_shape=[jax.ShapeDtypeStruct((n * tm, d), F32)] + out_shape,
        scratch_shapes=[scratch, pltpu.VMEM((d // LANES, tm, LANES), F32)],
        compiler_params=_cparams(("arbitrary",)),
    )(ctx, x, *proj[:4], *proj[4])


def _bwd_chunk(i, ncc, n):
    return jnp.where(i < ncc, ncc - 1 - i, n - 1 - i + ncc)


S5_LANE_CHUNK = 512
N_LRU_IN, N_S5_IN = 11, 10


def _lru_gates(d, c, u_c, u_p, u_n, cw_ref, cb_ref, wg_ref, bg_ref, lam_ref, a_scr, b_scr, ts, ncc, n):
    seg_start = jnp.logical_or(c == 0, c == ncc)
    seg_end = jnp.logical_or(c == ncc - 1, c == n - 1)
    prev = u_p[...] * jnp.where(seg_start, 0.0, 1.0)
    nxt = u_n[...] * jnp.where(seg_end, 0.0, 1.0)
    u = jnp.concatenate([prev, u_c[...], nxt], axis=0)
    cw = cw_ref[0]
    xc = cb_ref[0] + cw[0:1] * u[0:ts]
    for j in range(1, CONV_W):
        xc = xc + cw[j:j + 1] * u[j:j + ts]
    xc2 = xc.reshape(ts * xc.shape[1], W_LRU)
    gts = jnp.dot(xc2.astype(BF16), wg_ref[0, d], preferred_element_type=F32) + bg_ref[0, d]
    gate_r = jax.nn.sigmoid(gts[:, :W_LRU])
    gate_i = jax.nn.sigmoid(gts[:, W_LRU:])
    log_a = -LRU_C * gate_r * _softplus(-lam_ref[0, d])
    a = jnp.exp(log_a)
    th = jnp.tanh(log_a)
    one_minus_a2 = -2.0 * th / (1.0 - th)
    bb = jnp.sqrt(one_minus_a2) * (gate_i * xc2)
    a_scr[d] = a.reshape(xc.shape)
    b_scr[d] = bb.reshape(xc.shape)


def _s5_discretise(lre_ref, lim_ref, ldt_ref, bre_ref, bim_ref, bbar_scr, a_scr):
    bsz, nst = a_scr.shape[2:]
    for d in range(2):
        lre = lre_ref[0, d]
        lim = lim_ref[0, d]
        dt = jnp.exp(ldt_ref[0, d])
        mag = jnp.exp(lre * dt)
        ang = lim * dt
        ar = mag * jnp.cos(ang)
        ai = mag * jnp.sin(ang)
        nr = ar - 1.0
        den = lre * lre + lim * lim
        cfr = (nr * lre + ai * lim) / den
        cfi = (ai * lre - nr * lim) / den
        bre = bre_ref[0, d]
        bim = bim_ref[0, d]
        bbar_scr[d, :, :nst] = (cfr * bre - cfi * bim).astype(BF16)
        bbar_scr[d, :, nst:] = (cfr * bim + cfi * bre).astype(BF16)
        a_scr[d, 0] = jnp.broadcast_to(ar, (bsz, nst))
        a_scr[d, 1] = jnp.broadcast_to(ai, (bsz, nst))


def _scan_kernel(*refs, ts, ncc, n):
    lru_in, s5_in = refs[:N_LRU_IN], refs[N_LRU_IN:N_LRU_IN + N_S5_IN]
    hf_ref, hb_ref, yf_ref, yb_ref = refs[N_LRU_IN + N_S5_IN:N_LRU_IN + N_S5_IN + 4]
    la_scr, lb_scr, lcarry_scr, bbar_scr, sa_scr, buf_scr, scarry_scr = refs[N_LRU_IN + N_S5_IN + 4:]
    ufc, ufp, ufn, ubc, ubp, ubn, cw_ref, cb_ref, wg_ref, bg_ref, lam_ref = lru_in
    uf_ref, ub_ref, lre_ref, lim_ref, ldt_ref, bre_ref, bim_ref, cre_ref, cim_ref, dsk_ref = s5_in
    i = pl.program_id(0)
    nst = S5_N
    bsz = uf_ref.shape[1]

    @pl.when(i == 0)
    def _():
        lcarry_scr[...] = jnp.zeros_like(lcarry_scr)
        scarry_scr[...] = jnp.zeros_like(scarry_scr)
        _s5_discretise(lre_ref, lim_ref, ldt_ref, bre_ref, bim_ref, bbar_scr, sa_scr)

    for d, u_ref in enumerate((uf_ref, ub_ref)):
        u2 = u_ref[...].reshape(ts * bsz, W_S5).astype(BF16)
        bu = jnp.dot(u2, bbar_scr[d], preferred_element_type=F32)
        buf_scr[d] = bu.reshape(ts, bsz, 2 * nst)

    lru_par = (cw_ref, cb_ref, wg_ref, bg_ref, lam_ref, la_scr, lb_scr, ts, ncc, n)
    _lru_gates(0, i, ufc, ufp, ufn, *lru_par)
    _lru_gates(1, _bwd_chunk(i, ncc, n), ubc, ubp, ubn, *lru_par)
    hf = lcarry_scr[0]
    hb = lcarry_scr[1]
    for k in range(ts):
        kb = ts - 1 - k
        hf = la_scr[0, k] * hf + lb_scr[0, k]
        hb = la_scr[1, kb] * hb + lb_scr[1, kb]
        hf_ref[k] = hf
        hb_ref[kb] = hb
    lcarry_scr[0] = hf
    lcarry_scr[1] = hb

    lc = S5_LANE_CHUNK
    chains = [(d, pl.ds(c0, lc), pl.ds(nst + c0, lc)) for d in range(2) for c0 in range(0, nst, lc)]
    state = [(scarry_scr[d, :, re_sl], scarry_scr[d, :, im_sl]) for d, re_sl, im_sl in chains]
    for k in range(ts):
        for n_chain, (d, re_sl, im_sl) in enumerate(chains):
            t = k if d == 0 else ts - 1 - k
            ar = sa_scr[d, 0, :, re_sl]
            ai = sa_scr[d, 1, :, re_sl]
            sre, sim = state[n_chain]
            nre = ar * sre - ai * sim + buf_scr[d, t, :, re_sl]
            nim = ar * sim + ai * sre + buf_scr[d, t, :, im_sl]
            buf_scr[d, t, :, re_sl] = nre
            buf_scr[d, t, :, im_sl] = nim
            state[n_chain] = (nre, nim)
    for (d, re_sl, im_sl), (sre, sim) in zip(chains, state):
        scarry_scr[d, :, re_sl] = sre
        scarry_scr[d, :, im_sl] = sim

    for d, y_ref in enumerate((yf_ref, yb_ref)):
        st = buf_scr[d].reshape(ts * bsz, 2 * nst)
        y = (jnp.dot(st[:, :nst].astype(BF16), cre_ref[0, d], preferred_element_type=F32)
             - jnp.dot(st[:, nst:].astype(BF16), cim_ref[0, d], preferred_element_type=F32))
        y = y.reshape(ts, bsz, W_S5)
        y_ref[...] = dsk_ref[0] * uf_ref[...] + y if d == 0 else y


def _scans(pf3, layer, lru_params, s5_params, *, n_ctx):
    s_len, bsz, _ = pf3.shape
    ts = SCAN_TILE
    n = s_len // ts
    ncc = n_ctx // ts
    fwd = lambda i: i
    bwd = lambda i: _bwd_chunk(i, ncc, n)
    s5_col = W_LRU // W_S5

    def cur(cfn, col=0):
        return pl.BlockSpec((ts, bsz, W_LRU), lambda i: (cfn(i), 0, col))

    def prev(cfn):
        return pl.BlockSpec((2, bsz, W_LRU), lambda i: (jnp.maximum(cfn(i) * (ts // 2) - 1, 0), 0, 0))

    def nxt(cfn):
        return pl.BlockSpec((1, bsz, W_LRU), lambda i: (jnp.minimum((cfn(i) + 1) * ts, s_len - 1), 0, 0))

    const = lambda a: _layer_block(a, layer)
    assert len(lru_params) == N_LRU_IN - 6 and len(s5_params) == N_S5_IN - 2
    out_block = lambda cfn: pl.BlockSpec((ts, bsz, W_LRU), lambda i: (cfn(i), 0, 0))
    return pl.pallas_call(
        functools.partial(_scan_kernel, ts=ts, ncc=ncc, n=n),
        grid=(n,),
        in_specs=([cur(fwd), prev(fwd), nxt(fwd), cur(bwd), prev(bwd), nxt(bwd)]
                  + [const(a) for a in lru_params]
                  + [cur(fwd, s5_col), cur(bwd, s5_col)] + [const(a) for a in s5_params]),
        out_specs=[out_block(fwd), out_block(bwd), out_block(fwd), out_block(bwd)],
        out_shape=[jax.ShapeDtypeStruct((s_len, bsz, W_LRU), F32)] * 4,
        scratch_shapes=[pltpu.VMEM((2, ts, bsz, W_LRU), F32),
                        pltpu.VMEM((2, ts, bsz, W_LRU), F32),
                        pltpu.VMEM((2, bsz, W_LRU), F32),
                        pltpu.VMEM((2, W_S5, 2 * S5_N), BF16),
                        pltpu.VMEM((2, 2, bsz, S5_N), F32),
                        pltpu.VMEM((2, ts, bsz, 2 * S5_N), F32),
                        pltpu.VMEM((2, bsz, 2 * S5_N), F32)],
        compiler_params=_cparams(("arbitrary",)),
    )(*([pf3] * 6), *lru_params, pf3, pf3, *s5_params)


V_ROWS = DA_DV + 16
SCORE_PARTS = 2
SCORE_SLOTS = 4


def _attn_kernel(q_ref, k_ref, v_ref, dl_ref, g_ref, o_ref, s_scr, vt_scr, *, tq, n_ctx, lam_init):
    s_len = k_ref.shape[1]
    nq = s_len // tq
    ncq = n_ctx // tq
    for c0 in range(0, s_len, tq):
        vt_scr[0:DA_DV, c0:c0 + tq] = v_ref[0, c0:c0 + tq, :].astype(F32).T.astype(BF16)
    pad_row = lax.broadcasted_iota(jnp.int32, (V_ROWS - DA_DV, s_len), 0)
    vt_scr[DA_DV:, :] = jnp.where(pad_row == 0, 1.0, 0.0).astype(BF16)
    dl = dl_ref[0]
    lam = (jnp.exp(jnp.sum(dl[0:1] * dl[1:2], axis=-1, keepdims=True))
           - jnp.exp(jnp.sum(dl[2:3] * dl[3:4], axis=-1, keepdims=True)) + lam_init)
    gscale = g_ref[0] * (1.0 - lam_init)
    lane = lax.broadcasted_iota(jnp.int32, (tq, LANES), 1)

    def rows(qt):
        return pl.ds(pl.multiple_of(qt * tq, tq), tq)

    def key_parts(nk, max_parts):
        groups = nk // LANES
        n_parts = min(max_parts, groups)
        bounds = [LANES * ((groups * p) // n_parts) for p in range(n_parts + 1)]
        return tuple(zip(bounds[:-1], bounds[1:]))

    def scores(qt, mp, nk, slot):
        q = q_ref[0, rows(qt), :]
        qm = jnp.where((lane < DA_DH) if mp == 0 else (lane >= DA_DH), q, jnp.zeros_like(q))
        m = None
        for lo, hi in key_parts(nk, SCORE_PARTS):
            s_t = lax.dot_general(k_ref[0, lo:hi, :], qm, (((1,), (1,)), ((), ())),
                                  preferred_element_type=F32)
            s_scr[slot, lo:hi, :] = s_t
            m_part = jnp.max(s_t, axis=0, keepdims=True)
            m = m_part if m is None else jnp.maximum(m, m_part)
        return m

    def values(slot, m, nk):
        p_t = jnp.exp2(s_scr[slot, 0:nk, :] - m).astype(BF16)
        return jnp.dot(vt_scr[:, 0:nk], p_t, preferred_element_type=F32)

    def finish(qt, o1, o2):
        o_t = (o1[:DA_DV] * (1.0 / o1[DA_DV:DA_DV + 1])
               - o2[:DA_DV] * (lam / o2[DA_DV:DA_DV + 1]))
        y_t = o_t * lax.rsqrt(jnp.mean(o_t * o_t, axis=0, keepdims=True) + EPS)
        o_ref[0, rows(qt), :] = y_t.T * gscale

    for qt in range(ncq):
        m0 = scores(qt, 0, n_ctx, 0)
        m1 = scores(qt, 1, n_ctx, 2)
        finish(qt, values(0, m0, n_ctx), values(2, m1, n_ctx))

    def pair(t0, m0, last):
        m1 = scores(t0, 1, s_len, 2)
        o1 = values(0, m0, s_len)
        m0b = scores(t0 + 1, 0, s_len, 1)
        o2 = values(2, m1, s_len)
        finish(t0, o1, o2)
        m1b = scores(t0 + 1, 1, s_len, 3)
        o1b = values(1, m0b, s_len)
        m0_next = None if last else scores(t0 + 2, 0, s_len, 0)
        o2b = values(3, m1b, s_len)
        finish(t0 + 1, o1b, o2b)
        return m0_next

    n_pairs = (nq - ncq) // 2
    m0 = lax.fori_loop(0, n_pairs - 1, lambda j, m0: pair(ncq + 2 * j, m0, False),
                       scores(ncq, 0, s_len, 0))
    pair(nq - 2, m0, True)


def _attention(qk, v, layer, da_lam, da_g, *, n_ctx, lam_init):
    bsz, s_len, _ = qk.shape
    tq = Q_TILE
    assert n_ctx % tq == 0 and (s_len - n_ctx) % (2 * tq) == 0
    return pl.pallas_call(
        functools.partial(_attn_kernel, tq=tq, n_ctx=n_ctx, lam_init=lam_init),
        grid=(bsz, DA_HEADS),
        in_specs=[pl.BlockSpec((1, s_len, LANES), lambda b, h: (b, 0, DA_HEADS + h)),
                  pl.BlockSpec((1, s_len, LANES), lambda b, h: (b, 0, h)),
                  pl.BlockSpec((1, s_len, LANES), lambda b, h: (b, 0, h)),
                  _layer_block(da_lam, layer), _layer_block(da_g, layer)],
        out_specs=pl.BlockSpec((1, s_len, LANES), lambda b, h: (b, 0, h)),
        out_shape=jax.ShapeDtypeStruct((bsz, s_len, W_DA), F32),
        scratch_shapes=[pltpu.VMEM((SCORE_SLOTS, s_len, tq), F32),
                        pltpu.VMEM((V_ROWS, s_len), BF16)],
        compiler_params=_cparams(("arbitrary", "arbitrary")),
    )(qk, qk, v, da_lam, da_g)


def _gelu_tanh(x):
    return 0.5 * x * (1.0 + jnp.tanh(math.sqrt(2.0 / math.pi) * (x + 0.044715 * (x * x * x))))


N_MIX_REFS = 13


def _out_proj_kernel(*refs, final, ncb):
    (hf_ref, hb_ref, yf_ref, yb_ref, ga_ref, gs_ref, gd_ref, yd_ref, h_ref, mod_ref,
     wglu_ref, bglu_ref, wo_ref) = refs[:N_MIX_REFS]
    yd_scr = refs[-1]
    y_a = (hf_ref[...] + hb_ref[...]) * _silu(ga_ref[...])
    z = _gelu_tanh(yf_ref[...] + yb_ref[...])
    glu = jax.nn.sigmoid(jnp.dot(z.astype(BF16), wglu_ref[0], preferred_element_type=F32)
                         + bglu_ref[0])
    y_s = z * glu * _silu(gs_ref[...])
    y_d = _batch_to_rows(yd_ref, yd_scr) * _silu(gd_ref[...])
    out = (jnp.dot(y_a.astype(BF16), wo_ref[0, 0:W_LRU, :], preferred_element_type=F32)
           + jnp.dot(y_s.astype(BF16), wo_ref[0, W_LRU:W_LRU + W_S5, :], preferred_element_type=F32)
           + jnp.dot(y_d.astype(BF16), wo_ref[0, W_LRU + W_S5:, :], preferred_element_type=F32))
    rows, d = out.shape
    gate = mod_ref[0, 0, 2]
    hn = h_ref[...] + (out.reshape(rows // BATCH, BATCH, d) * gate[None]).reshape(rows, d)
    if final:
        fg_ref, o_ref, out_scr = refs[N_MIX_REFS:-1]
        hn = hn * lax.rsqrt(jnp.mean(hn * hn, axis=-1, keepdims=True) + EPS) * fg_ref[...]
        _rows_to_batch(hn, out_scr, 0, o_ref, 0, F32)
    else:
        n_proj_in = 8
        hs_ref = refs[N_MIX_REFS + n_proj_in]
        hs_ref[...] = hn
        _project(pl.program_id(0), hn, *refs[N_MIX_REFS:N_MIX_REFS + n_proj_in],
                 *refs[N_MIX_REFS + n_proj_in + 1:-1], ncb)


def _out_proj(hf, hb, yf, yb, pf, yd, hs, layer, mods, w_glu, b_glu, w_out, *, n_ctx,
              final_g=None, next_proj=None):
    final = final_g is not None
    n_rows, d = hs.shape
    tt = TOKEN_TILE
    tm = tt * BATCH
    ncb = n_ctx // tt
    off = ncb if final else 0
    nblk = n_rows // tm - off
    blk = lambda w, c=0: pl.BlockSpec((tm, w), lambda i: (i + off, c))
    const = lambda a: _layer_block(a, layer)
    kind = lambda i: (i + off >= ncb).astype(jnp.int32)
    in_specs = [blk(W_LRU), blk(W_LRU), blk(W_S5), blk(W_S5),
                blk(W_LRU, 2), blk(W_S5, 3), blk(W_DA, 2),
                pl.BlockSpec((BATCH, tt, W_DA), lambda i: (0, i + off, 0)), blk(d),
                pl.BlockSpec((1, 1, 3, BATCH, d), lambda i: (layer, kind(i), 0, 0, 0)),
                const(w_glu), const(b_glu), const(w_out)]
    operands = [hf, hb, yf, yb, pf, pf, pf, yd, hs, mods, w_glu, b_glu, w_out]
    assert len(operands) == N_MIX_REFS
    yd_scratch = pltpu.VMEM((W_DA // LANES, tm, LANES), F32)
    if final:
        final_g = final_g.reshape(1, d)
        in_specs += [pl.BlockSpec(final_g.shape, lambda i: (0, 0))]
        operands += [final_g]
        out_specs = pl.BlockSpec((BATCH, tt, d), lambda i: (0, i, 0))
        out_shape = jax.ShapeDtypeStruct((BATCH, nblk * tt, d), F32)
        scratch = [pltpu.VMEM((d // LANES, tm, LANES), F32), yd_scratch]
    else:
        p_in, p_out, p_shape, p_scratch = _project_specs(layer + 1, next_proj, tt, n_rows, kind)
        in_specs += p_in
        operands += [*next_proj[:4], *next_proj[4]]
        out_specs = [pl.BlockSpec((tm, d), lambda i: (i, 0))] + p_out
        out_shape = [jax.ShapeDtypeStruct((n_rows, d), F32)] + p_shape
        scratch = [p_scratch, yd_scratch]
    return pl.pallas_call(
        functools.partial(_out_proj_kernel, final=final, ncb=ncb),
        grid=(nblk,),
        in_specs=in_specs,
        out_specs=out_specs,
        out_shape=out_shape,
        scratch_shapes=scratch,
        compiler_params=_cparams(("arbitrary",)),
    )(*operands)


def _block_diag(w):
    nb, k, n = w.shape[-3:]
    rows = w.reshape(w.shape[:-3] + (nb * k, n))
    tiled = jnp.tile(rows, (1,) * (rows.ndim - 1) + (nb,))
    row_blk = lax.broadcasted_iota(jnp.int32, (nb * k, nb * n), 0) // k
    col_blk = lax.broadcasted_iota(jnp.int32, (nb * k, nb * n), 1) // n
    return jnp.where(row_blk == col_blk, tiled, jnp.zeros_like(tiled))


def kernel(x, c, ctx, c_ctx, norm_g, w_mod, b_mod, w_in, w_out, lru_conv_w, lru_conv_b, lru_wa, lru_ba,
           lru_wx, lru_bx, lru_lam, s5_lam_re, s5_lam_im, s5_log_dt, s5_b_re, s5_b_im, s5_c_re, s5_c_im,
           s5_d, s5_w_glu, s5_b_glu, da_lam, da_norm_g, final_g):
    bsz, t_len, d = x.shape
    n_ctx = ctx.shape[1]
    s_len = n_ctx + t_len
    n_rows = s_len * bsz
    n_layers = w_in.shape[0]
    assert bsz == BATCH and bsz + 1 <= MOD_ROWS
    assert n_ctx % Q_TILE == 0 and t_len % Q_TILE == 0
    assert n_ctx % TOKEN_TILE == 0 and n_ctx % SCAN_TILE == 0

    cc = jnp.concatenate([c, c_ctx[None], jnp.zeros((MOD_ROWS - bsz - 1, d), F32)], axis=0)
    mod = _modulation(cc, w_mod, b_mod).reshape(n_layers, MOD_ROWS, 3, d)
    mod_lat = mod[:, :bsz].transpose(0, 2, 1, 3)
    mod_ctx = jnp.broadcast_to(mod[:, bsz:bsz + 1], (n_layers, bsz, 3, d)).transpose(0, 2, 1, 3)
    mods = jnp.stack([mod_ctx, mod_lat], axis=1)

    tables = _rope_tables()

    side = W_LRU + W_S5 + 2 * W_DA
    c_ua, c_us = (0, W_LRU), (W_LRU, W_LRU + W_S5)
    c_k, c_v = (W_LRU + W_S5, W_LRU + W_S5 + W_DA), (W_LRU + W_S5 + W_DA, side)
    sl = lambda lo_hi, base=0: w_in[:, :, base + lo_hi[0]:base + lo_hi[1]]
    wf = jnp.concatenate([sl(c_ua), sl(c_us), sl(c_ua, side), sl(c_us, side), sl(c_v, side)],
                         axis=-1).astype(BF16)
    wa = jnp.concatenate([sl(c_k), sl(c_v), sl(c_k, side)], axis=-1).astype(BF16)
    w_out_b = w_out.astype(BF16)
    w_glu_b = s5_w_glu.astype(BF16)

    bd = _block_diag
    wg = jnp.concatenate([bd(lru_wa), bd(lru_wx)], axis=-1).astype(BF16)
    bg = jnp.concatenate([lru_ba, lru_bx], axis=-1).reshape(n_layers, 2, 1, 2 * W_LRU)
    lru_lam_r = lru_lam.reshape(n_layers, 2, 1, W_LRU)
    lru_conv_b_r = lru_conv_b.reshape(n_layers, 1, W_LRU)
    flat = lambda a: a.reshape(n_layers, 2, 1, S5_N)
    ldt = jnp.broadcast_to(s5_log_dt[..., None], (n_layers, 2, S5_G, S5_P))
    bre = bd(jnp.swapaxes(s5_b_re, -1, -2))
    bim = bd(jnp.swapaxes(s5_b_im, -1, -2))
    cre = bd(jnp.swapaxes(s5_c_re, -1, -2)).astype(BF16)
    cim = bd(jnp.swapaxes(s5_c_im, -1, -2)).astype(BF16)
    lam_re, lam_im, ldt = flat(s5_lam_re), flat(s5_lam_im), flat(ldt)

    lru_params = (lru_conv_w, lru_conv_b_r, wg, bg, lru_lam_r)
    s5_params = (lam_re, lam_im, ldt, bre, bim, cre, cim, s5_d.reshape(n_layers, 1, W_S5))
    proj = (mods, norm_g.reshape(n_layers, 1, d), wf, wa, tables)
    da_lam_r = da_lam.reshape(n_layers, 4, DA_DH)
    da_g_r = da_norm_g.reshape(n_layers, 1, DA_DV)
    b_glu_r = s5_b_glu.reshape(n_layers, 1, W_S5)

    hs, pf, kq, v = _in_proj(ctx, x, proj)
    for l in range(n_layers):
        final = l == n_layers - 1
        lam_init = 0.8 - 0.6 * math.exp(-0.3 * l)
        pf3 = pf.reshape(s_len, bsz, PF_W)

        hf, hb, yf, yb = _scans(pf3, l, lru_params, s5_params, n_ctx=n_ctx)

        yd = _attention(kq, v, l, da_lam_r, da_g_r, n_ctx=n_ctx, lam_init=lam_init)

        mix = (hf.reshape(n_rows, W_LRU), hb.reshape(n_rows, W_LRU),
               yf.reshape(n_rows, W_S5), yb.reshape(n_rows, W_S5),
               pf, yd, hs, l, mods, w_glu_b, b_glu_r, w_out_b)
        if final:
            return _out_proj(*mix, n_ctx=n_ctx, final_g=final_g)
        hs, pf, kq, v = _out_proj(*mix, n_ctx=n_ctx, next_proj=proj)
```

```python
import functools
import math

import jax
import jax.numpy as jnp
from jax import lax
from jax.experimental import pallas as pl
from jax.experimental.pallas import tpu as pltpu

F32 = jnp.float32
BF16 = jnp.bfloat16

GRID_W = 64
EPS = 1e-6
W_LRU = 256
LRU_BLOCKS = 4
CONV_W = 4
LRU_C = 8.0
W_S5 = 256
S5_H = 16
S5_G = 16
S5_P = 64
S5_N = S5_G * S5_P
DA_HEADS = 4
DA_DH = 64
DA_DV = 2 * DA_DH
W_DA = DA_HEADS * DA_DV
ROPE_BASE = 10000.0
ROPE_F = DA_DH // 4
LANES = 128
BATCH = 8

PF_W = 2 * W_LRU + 2 * W_S5
PA_W = 4 * W_DA

TOKEN_TILE = 64
PROJ_PARTS = 2
SCAN_TILE = 64
Q_TILE = 256
MOD_ROWS = 16
VMEM_LIMIT = 48 * 1024 * 1024


def _layer_block(a, layer):
    index = (layer,) + (0,) * (a.ndim - 1)
    return pl.BlockSpec((1,) + a.shape[1:], lambda *_: index, pipeline_mode=pl.Buffered(1))


def _cparams(sem):
    return pltpu.CompilerParams(dimension_semantics=sem, vmem_limit_bytes=VMEM_LIMIT)


def _silu(x):
    return x * jax.nn.sigmoid(x)


def _softplus(x):
    return jnp.maximum(x, 0.0) + jnp.log1p(jnp.exp(-jnp.abs(x)))


def _rows_to_batch(val, scr, slab0, out_ref, col0, dtype, t0=0):
    rows, width = val.shape
    tt = rows // BATCH
    r0 = t0 * BATCH
    for j in range(width // LANES):
        scr[slab0 + j, r0:r0 + rows, :] = val[:, j * LANES:(j + 1) * LANES]
    for b in range(BATCH):
        for j in range(width // LANES):
            out_ref[b, t0:t0 + tt, col0 + j * LANES:col0 + (j + 1) * LANES] = (
                scr[slab0 + j, pl.ds(r0 + b, tt, stride=BATCH), :].astype(dtype))


def _batch_to_rows(in_ref, scr):
    _, tt, width = in_ref.shape
    for b in range(BATCH):
        for j in range(width // LANES):
            scr[j, pl.ds(b, tt, stride=BATCH), :] = in_ref[b, :, j * LANES:(j + 1) * LANES].astype(F32)
    return jnp.concatenate([scr[j] for j in range(width // LANES)], axis=1)


def _rope_table_kernel(csp_ref, snp_ref, csr_ref, snr_ref):
    def table(shape, pos_shift):
        pos = lax.broadcasted_iota(jnp.int32, shape, 0) >> pos_shift
        lane = lax.broadcasted_iota(jnp.int32, shape, 1)
        f = (lane & (ROPE_F - 1)).astype(F32)
        ang = pos.astype(F32) * jnp.exp(-math.log(ROPE_BASE) * f / ROPE_F)
        sn = jnp.sin(ang)
        return jnp.cos(ang), jnp.where((lane & ROPE_F) == 0, -sn, sn)

    csp_ref[...], snp_ref[...] = table(csp_ref.shape, 0)
    csr_ref[...], snr_ref[...] = table(csr_ref.shape, int(math.log2(BATCH)))


def _rope_tables():
    shapes = [(GRID_W, LANES)] * 2 + [(GRID_W * BATCH, LANES)] * 2
    return pl.pallas_call(
        _rope_table_kernel,
        out_shape=[jax.ShapeDtypeStruct(s, F32) for s in shapes],
        compiler_params=pltpu.CompilerParams(vmem_limit_bytes=VMEM_LIMIT),
    )()


def _mod_kernel(c_ref, w_ref, b_ref, o_ref):
    c = c_ref[...]
    o_ref[0] = (jnp.dot(_silu(c).astype(BF16), w_ref[0].astype(BF16), preferred_element_type=F32)
                + b_ref[0])


def _modulation(cc, w_mod, b_mod):
    n_layers, d, d3 = w_mod.shape
    return pl.pallas_call(
        _mod_kernel,
        grid=(n_layers, d3 // d),
        in_specs=[pl.BlockSpec((MOD_ROWS, d), lambda l, j: (0, 0)),
                  pl.BlockSpec((1, d, d), lambda l, j: (l, 0, j)),
                  pl.BlockSpec((1, 1, d), lambda l, j: (l, 0, j))],
        out_specs=pl.BlockSpec((1, MOD_ROWS, d), lambda l, j: (l, 0, j)),
        out_shape=jax.ShapeDtypeStruct((n_layers, MOD_ROWS, d3), F32),
        compiler_params=_cparams(("arbitrary", "arbitrary")),
    )(cc, w_mod, b_mod.reshape(n_layers, 1, d3))


def _rope(x, cos, sin):
    width = x.shape[1]
    reps = width // LANES
    cos = jnp.concatenate([cos] * reps, axis=1)
    sin = jnp.concatenate([sin] * reps, axis=1)
    lane = lax.broadcasted_iota(jnp.int32, x.shape, 1)
    first = (lane & ROPE_F) == 0
    partner = jnp.where(first, pltpu.roll(x, width - ROPE_F, 1), pltpu.roll(x, ROPE_F, 1))
    return x * cos + partner * sin


def _project(i, x, mod_ref, g_ref, wf_ref, wa_ref, csp_ref, snp_ref, csr_ref, snr_ref,
             pf_ref, kq_ref, v_ref, gd_ref, scr, ncb):
    rows_all, d = x.shape
    rows = rows_all // PROJ_PARTS
    tt = rows // BATCH
    shift = mod_ref[0, 0, 0]
    scale = mod_ref[0, 0, 1]
    grid_row = jnp.maximum(i - ncb, 0)
    is_ctx = i < ncb
    lane = lax.broadcasted_iota(jnp.int32, (rows, LANES), 1)
    use_col = (lane & (DA_DH // 2)) != 0
    slabs = W_DA // LANES
    for part in range(PROJ_PARTS):
        r0 = part * rows
        xp = x[r0:r0 + rows]
        y = xp * lax.rsqrt(jnp.mean(xp * xp, axis=-1, keepdims=True) + EPS) * g_ref[0]
        y3 = y.reshape(tt, BATCH, d)
        n = (y3 * (1.0 + scale)[None] + shift[None]).reshape(rows, d).astype(BF16)
        pf_ref[r0:r0 + rows, :] = jnp.dot(n, wf_ref[0], preferred_element_type=F32)
        att = jnp.dot(n, wa_ref[0], preferred_element_type=F32)
        cos = jnp.where(use_col, csr_ref[r0:r0 + rows, :], csp_ref[pl.ds(grid_row, 1), :])
        sin = jnp.where(use_col, snr_ref[r0:r0 + rows, :], snp_ref[pl.ds(grid_row, 1), :])
        cos = jnp.where(is_ctx, 1.0, cos)
        sin = jnp.where(is_ctx, 0.0, sin)
        k = _rope(att[:, :W_DA], cos, sin)
        v = att[:, W_DA:2 * W_DA]
        q = _rope(att[:, 2 * W_DA:3 * W_DA], cos, sin) * (DA_DH ** -0.5 * math.log2(math.e))
        _rows_to_batch(k, scr, 0, kq_ref, 0, BF16, part * tt)
        _rows_to_batch(q, scr, slabs, kq_ref, W_DA, BF16, part * tt)
        _rows_to_batch(v, scr, 2 * slabs, v_ref, 0, BF16, part * tt)
        _rows_to_batch(att[:, 3 * W_DA:], scr, 3 * slabs, gd_ref, 0, F32, part * tt)


def _in_proj_kernel(ctx_ref, x_ref, *refs, ncb):
    proj_in, (hs_ref, pf_ref, kq_ref, v_ref, gd_ref, scr, rows_scr) = refs[:8], refs[8:]
    i = pl.program_id(0)

    @pl.when(i < ncb)
    def _():
        hs_ref[...] = _batch_to_rows(ctx_ref, rows_scr)

    @pl.when(i >= ncb)
    def _():
        hs_ref[...] = _batch_to_rows(x_ref, rows_scr)

    _project(i, hs_ref[...], *proj_in, pf_ref, kq_ref, v_ref, gd_ref, scr, ncb)


def _project_specs(layer, proj, tt, n_rows, kind):
    mods, g, wf, wa, tables = proj
    d = wf.shape[1]
    tm = tt * BATCH
    s_len = n_rows // BATCH
    const = lambda shape: pl.BlockSpec(shape, lambda i: (0,) * len(shape),
                                       pipeline_mode=pl.Buffered(1))
    in_specs = [pl.BlockSpec((1, 1, 3, BATCH, d), lambda i: (layer, kind(i), 0, 0, 0)),
                _layer_block(g, layer), _layer_block(wf, layer), _layer_block(wa, layer)
                ] + [const(t.shape) for t in tables]
    out_specs = [pl.BlockSpec((tm, PF_W), lambda i: (i, 0)),
                 pl.BlockSpec((BATCH, tt, 2 * W_DA), lambda i: (0, i, 0)),
                 pl.BlockSpec((BATCH, tt, W_DA), lambda i: (0, i, 0)),
                 pl.BlockSpec((BATCH, tt, W_DA), lambda i: (0, i, 0))]
    out_shape = [jax.ShapeDtypeStruct((n_rows, PF_W), F32),
                 jax.ShapeDtypeStruct((BATCH, s_len, 2 * W_DA), BF16),
                 jax.ShapeDtypeStruct((BATCH, s_len, W_DA), BF16),
                 jax.ShapeDtypeStruct((BATCH, s_len, W_DA), F32)]
    scratch = pltpu.VMEM((PA_W // LANES, tm, LANES), F32)
    return in_specs, out_specs, out_shape, scratch


def _in_proj(ctx, x, proj):
    bsz, n_ctx, d = ctx.shape
    tt = TOKEN_TILE
    assert tt == GRID_W
    tm = tt * BATCH
    ncb = n_ctx // tt
    n = ncb + x.shape[1] // tt
    kind = lambda i: (i >= ncb).astype(jnp.int32)
    in_specs, out_specs, out_shape, scratch = _project_specs(0, proj, tt, n * tm, kind)
    return pl.pallas_call(
        functools.partial(_in_proj_kernel, ncb=ncb),
        grid=(n,),
        in_specs=[pl.BlockSpec((bsz, tt, d), lambda i: (0, jnp.minimum(i, ncb - 1), 0)),
                  pl.BlockSpec((bsz, tt, d), lambda i: (0, jnp.maximum(i - ncb, 0), 0))] + in_specs,
        out_specs=[pl.BlockSpec((tm, d), lambda i: (i, 0))] + out_specs,
        out_shape=[jax.ShapeDtypeStruct((n * tm, d), F32)] + out_shape,
        scratch_shapes=[scratch, pltpu.VMEM((d // LANES, tm, LANES), F32)],
        compiler_params=_cparams(("arbitrary",)),
    )(ctx, x, *proj[:4], *proj[4])


def _bwd_chunk(i, ncc, n):
    return jnp.where(i < ncc, ncc - 1 - i, n - 1 - i + ncc)


S5_LANE_CHUNK = 512
N_LRU_IN, N_S5_IN = 11, 10


def _lru_gates(d, c, u_c, u_p, u_n, cw_ref, cb_ref, wg_ref, bg_ref, lam_ref, a_scr, b_scr, ts, ncc, n):
    seg_start = jnp.logical_or(c == 0, c == ncc)
    seg_end = jnp.logical_or(c == ncc - 1, c == n - 1)
    prev = u_p[...] * jnp.where(seg_start, 0.0, 1.0)
    nxt = u_n[...] * jnp.where(seg_end, 0.0, 1.0)
    u = jnp.concatenate([prev, u_c[...], nxt], axis=0)
    cw = cw_ref[0]
    xc = cb_ref[0] + cw[0:1] * u[0:ts]
    for j in range(1, CONV_W):
        xc = xc + cw[j:j + 1] * u[j:j + ts]
    xc2 = xc.reshape(ts * xc.shape[1], W_LRU)
    gts = jnp.dot(xc2.astype(BF16), wg_ref[0, d], preferred_element_type=F32) + bg_ref[0, d]
    gate_r = jax.nn.sigmoid(gts[:, :W_LRU])
    gate_i = jax.nn.sigmoid(gts[:, W_LRU:])
    log_a = -LRU_C * gate_r * _softplus(-lam_ref[0, d])
    a = jnp.exp(log_a)
    th = jnp.tanh(log_a)
    one_minus_a2 = -2.0 * th / (1.0 - th)
    bb = jnp.sqrt(one_minus_a2) * (gate_i * xc2)
    a_scr[d] = a.reshape(xc.shape)
    b_scr[d] = bb.reshape(xc.shape)


def _s5_discretise(lre_ref, lim_ref, ldt_ref, bre_ref, bim_ref, bbar_scr, a_scr):
    bsz, nst = a_scr.shape[2:]
    for d in range(2):
        lre = lre_ref[0, d]
        lim = lim_ref[0, d]
        dt = jnp.exp(ldt_ref[0, d])
        mag = jnp.exp(lre * dt)
        ang = lim * dt
        ar = mag * jnp.cos(ang)
        ai = mag * jnp.sin(ang)
        nr = ar - 1.0
        den = lre * lre + lim * lim
        cfr = (nr * lre + ai * lim) / den
        cfi = (ai * lre - nr * lim) / den
        bre = bre_ref[0, d]
        bim = bim_ref[0, d]
        bbar_scr[d, :, :nst] = (cfr * bre - cfi * bim).astype(BF16)
        bbar_scr[d, :, nst:] = (cfr * bim + cfi * bre).astype(BF16)
        a_scr[d, 0] = jnp.broadcast_to(ar, (bsz, nst))
        a_scr[d, 1] = jnp.broadcast_to(ai, (bsz, nst))


def _scan_kernel(*refs, ts, ncc, n):
    lru_in, s5_in = refs[:N_LRU_IN], refs[N_LRU_IN:N_LRU_IN + N_S5_IN]
    hf_ref, hb_ref, yf_ref, yb_ref = refs[N_LRU_IN + N_S5_IN:N_LRU_IN + N_S5_IN + 4]
    la_scr, lb_scr, lcarry_scr, bbar_scr, sa_scr, buf_scr, scarry_scr = refs[N_LRU_IN + N_S5_IN + 4:]
    ufc, ufp, ufn, ubc, ubp, ubn, cw_ref, cb_ref, wg_ref, bg_ref, lam_ref = lru_in
    uf_ref, ub_ref, lre_ref, lim_ref, ldt_ref, bre_ref, bim_ref, cre_ref, cim_ref, dsk_ref = s5_in
    i = pl.program_id(0)
    nst = S5_N
    bsz = uf_ref.shape[1]

    @pl.when(i == 0)
    def _():
        lcarry_scr[...] = jnp.zeros_like(lcarry_scr)
        scarry_scr[...] = jnp.zeros_like(scarry_scr)
        _s5_discretise(lre_ref, lim_ref, ldt_ref, bre_ref, bim_ref, bbar_scr, sa_scr)

    for d, u_ref in enumerate((uf_ref, ub_ref)):
        u2 = u_ref[...].reshape(ts * bsz, W_S5).astype(BF16)
        bu = jnp.dot(u2, bbar_scr[d], preferred_element_type=F32)
        buf_scr[d] = bu.reshape(ts, bsz, 2 * nst)

    lru_par = (cw_ref, cb_ref, wg_ref, bg_ref, lam_ref, la_scr, lb_scr, ts, ncc, n)
    _lru_gates(0, i, ufc, ufp, ufn, *lru_par)
    _lru_gates(1, _bwd_chunk(i, ncc, n), ubc, ubp, ubn, *lru_par)
    hf = lcarry_scr[0]
    hb = lcarry_scr[1]
    for k in range(ts):
        kb = ts - 1 - k
        hf = la_scr[0, k] * hf + lb_scr[0, k]
        hb = la_scr[1, kb] * hb + lb_scr[1, kb]
        hf_ref[k] = hf
        hb_ref[kb] = hb
    lcarry_scr[0] = hf
    lcarry_scr[1] = hb

    lc = S5_LANE_CHUNK
    chains = [(d, pl.ds(c0, lc), pl.ds(nst + c0, lc)) for d in range(2) for c0 in range(0, nst, lc)]
    state = [(scarry_scr[d, :, re_sl], scarry_scr[d, :, im_sl]) for d, re_sl, im_sl in chains]
    for k in range(ts):
        for n_chain, (d, re_sl, im_sl) in enumerate(chains):
            t = k if d == 0 else ts - 1 - k
            ar = sa_scr[d, 0, :, re_sl]
            ai = sa_scr[d, 1, :, re_sl]
            sre, sim = state[n_chain]
            nre = ar * sre - ai * sim + buf_scr[d, t, :, re_sl]
            nim = ar * sim + ai * sre + buf_scr[d, t, :, im_sl]
            buf_scr[d, t, :, re_sl] = nre
            buf_scr[d, t, :, im_sl] = nim
            state[n_chain] = (nre, nim)
    for (d, re_sl, im_sl), (sre, sim) in zip(chains, state):
        scarry_scr[d, :, re_sl] = sre
        scarry_scr[d, :, im_sl] = sim

    for d, y_ref in enumerate((yf_ref, yb_ref)):
        st = buf_scr[d].reshape(ts * bsz, 2 * nst)
        y = (jnp.dot(st[:, :nst].astype(BF16), cre_ref[0, d], preferred_element_type=F32)
             - jnp.dot(st[:, nst:].astype(BF16), cim_ref[0, d], preferred_element_type=F32))
        y = y.reshape(ts, bsz, W_S5)
        y_ref[...] = dsk_ref[0] * uf_ref[...] + y if d == 0 else y


def _scans(pf3, layer, lru_params, s5_params, *, n_ctx):
    s_len, bsz, _ = pf3.shape
    ts = SCAN_TILE
    n = s_len // ts
    ncc = n_ctx // ts
    fwd = lambda i: i
    bwd = lambda i: _bwd_chunk(i, ncc, n)
    s5_col = W_LRU // W_S5

    def cur(cfn, col=0):
        return pl.BlockSpec((ts, bsz, W_LRU), lambda i: (cfn(i), 0, col))

    def prev(cfn):
        return pl.BlockSpec((2, bsz, W_LRU), lambda i: (jnp.maximum(cfn(i) * (ts // 2) - 1, 0), 0, 0))

    def nxt(cfn):
        return pl.BlockSpec((1, bsz, W_LRU), lambda i: (jnp.minimum((cfn(i) + 1) * ts, s_len - 1), 0, 0))

    const = lambda a: _layer_block(a, layer)
    assert len(lru_params) == N_LRU_IN - 6 and len(s5_params) == N_S5_IN - 2
    out_block = lambda cfn: pl.BlockSpec((ts, bsz, W_LRU), lambda i: (cfn(i), 0, 0))
    return pl.pallas_call(
        functools.partial(_scan_kernel, ts=ts, ncc=ncc, n=n),
        grid=(n,),
        in_specs=([cur(fwd), prev(fwd), nxt(fwd), cur(bwd), prev(bwd), nxt(bwd)]
                  + [const(a) for a in lru_params]
                  + [cur(fwd, s5_col), cur(bwd, s5_col)] + [const(a) for a in s5_params]),
        out_specs=[out_block(fwd), out_block(bwd), out_block(fwd), out_block(bwd)],
        out_shape=[jax.ShapeDtypeStruct((s_len, bsz, W_LRU), F32)] * 4,
        scratch_shapes=[pltpu.VMEM((2, ts, bsz, W_LRU), F32),
                        pltpu.VMEM((2, ts, bsz, W_LRU), F32),
                        pltpu.VMEM((2, bsz, W_LRU), F32),
                        pltpu.VMEM((2, W_S5, 2 * S5_N), BF16),
                        pltpu.VMEM((2, 2, bsz, S5_N), F32),
                        pltpu.VMEM((2, ts, bsz, 2 * S5_N), F32),
                        pltpu.VMEM((2, bsz, 2 * S5_N), F32)],
        compiler_params=_cparams(("arbitrary",)),
    )(*([pf3] * 6), *lru_params, pf3, pf3, *s5_params)


V_ROWS = DA_DV + 16
SCORE_PARTS = 2
SCORE_SLOTS = 4


def _attn_kernel(q_ref, k_ref, v_ref, gd_ref, dl_ref, g_ref, o_ref, s_scr, vt_scr, *, tq, n_ctx,
                 lam_init):
    s_len = k_ref.shape[1]
    nq = s_len // tq
    ncq = n_ctx // tq
    for c0 in range(0, s_len, tq):
        vt_scr[0:DA_DV, c0:c0 + tq] = v_ref[0, c0:c0 + tq, :].astype(F32).T.astype(BF16)
    pad_row = lax.broadcasted_iota(jnp.int32, (V_ROWS - DA_DV, s_len), 0)
    vt_scr[DA_DV:, :] = jnp.where(pad_row == 0, 1.0, 0.0).astype(BF16)
    dl = dl_ref[0]
    lam = (jnp.exp(jnp.sum(dl[0:1] * dl[1:2], axis=-1, keepdims=True))
           - jnp.exp(jnp.sum(dl[2:3] * dl[3:4], axis=-1, keepdims=True)) + lam_init)
    gscale = g_ref[0] * (1.0 - lam_init)
    lane = lax.broadcasted_iota(jnp.int32, (tq, LANES), 1)

    def rows(qt):
        return pl.ds(pl.multiple_of(qt * tq, tq), tq)

    def key_parts(nk, max_parts):
        groups = nk // LANES
        n_parts = min(max_parts, groups)
        bounds = [LANES * ((groups * p) // n_parts) for p in range(n_parts + 1)]
        return tuple(zip(bounds[:-1], bounds[1:]))

    def scores(qt, mp, nk, slot):
        q = q_ref[0, rows(qt), :]
        qm = jnp.where((lane < DA_DH) if mp == 0 else (lane >= DA_DH), q, jnp.zeros_like(q))
        m = None
        for lo, hi in key_parts(nk, SCORE_PARTS):
            s_t = lax.dot_general(k_ref[0, lo:hi, :], qm, (((1,), (1,)), ((), ())),
                                  preferred_element_type=F32)
            s_scr[slot, lo:hi, :] = s_t
            m_part = jnp.max(s_t, axis=0, keepdims=True)
            m = m_part if m is None else jnp.maximum(m, m_part)
        return m

    def values(slot, m, nk):
        p_t = jnp.exp2(s_scr[slot, 0:nk, :] - m).astype(BF16)
        return jnp.dot(vt_scr[:, 0:nk], p_t, preferred_element_type=F32)

    def finish(qt, o1, o2):
        o_t = (o1[:DA_DV] * (1.0 / o1[DA_DV:DA_DV + 1])
               - o2[:DA_DV] * (lam / o2[DA_DV:DA_DV + 1]))
        y_t = o_t * lax.rsqrt(jnp.mean(o_t * o_t, axis=0, keepdims=True) + EPS)
        y = y_t.T * gscale * _silu(gd_ref[0, rows(qt), :])
        o_ref[0, rows(qt), :] = y.astype(o_ref.dtype)

    for qt in range(ncq):
        m0 = scores(qt, 0, n_ctx, 0)
        m1 = scores(qt, 1, n_ctx, 2)
        finish(qt, values(0, m0, n_ctx), values(2, m1, n_ctx))

    def pair(t0, m0, last):
        m1 = scores(t0, 1, s_len, 2)
        o1 = values(0, m0, s_len)
        m0b = scores(t0 + 1, 0, s_len, 1)
        o2 = values(2, m1, s_len)
        finish(t0, o1, o2)
        m1b = scores(t0 + 1, 1, s_len, 3)
        o1b = values(1, m0b, s_len)
        m0_next = None if last else scores(t0 + 2, 0, s_len, 0)
        o2b = values(3, m1b, s_len)
        finish(t0 + 1, o1b, o2b)
        return m0_next

    n_pairs = (nq - ncq) // 2
    m0 = lax.fori_loop(0, n_pairs - 1, lambda j, m0: pair(ncq + 2 * j, m0, False),
                       scores(ncq, 0, s_len, 0))
    pair(nq - 2, m0, True)


def _attention(qk, v, gd, layer, da_lam, da_g, *, n_ctx, lam_init):
    bsz, s_len, _ = qk.shape
    tq = Q_TILE
    assert n_ctx % tq == 0 and (s_len - n_ctx) % (2 * tq) == 0
    return pl.pallas_call(
        functools.partial(_attn_kernel, tq=tq, n_ctx=n_ctx, lam_init=lam_init),
        grid=(bsz, DA_HEADS),
        in_specs=[pl.BlockSpec((1, s_len, LANES), lambda b, h: (b, 0, DA_HEADS + h)),
                  pl.BlockSpec((1, s_len, LANES), lambda b, h: (b, 0, h)),
                  pl.BlockSpec((1, s_len, LANES), lambda b, h: (b, 0, h)),
                  pl.BlockSpec((1, s_len, LANES), lambda b, h: (b, 0, h)),
                  _layer_block(da_lam, layer), _layer_block(da_g, layer)],
        out_specs=pl.BlockSpec((1, s_len, LANES), lambda b, h: (b, 0, h)),
        out_shape=jax.ShapeDtypeStruct((bsz, s_len, W_DA), BF16),
        scratch_shapes=[pltpu.VMEM((SCORE_SLOTS, s_len, tq), F32),
                        pltpu.VMEM((V_ROWS, s_len), BF16)],
        compiler_params=_cparams(("arbitrary", "arbitrary")),
    )(qk, qk, v, gd, da_lam, da_g)


def _gelu_tanh(x):
    return 0.5 * x * (1.0 + jnp.tanh(math.sqrt(2.0 / math.pi) * (x + 0.044715 * (x * x * x))))


N_MIX_REFS = 12


def _out_proj_kernel(*refs, final, ncb):
    (hf_ref, hb_ref, yf_ref, yb_ref, ga_ref, gs_ref, yd_ref, h_ref, mod_ref,
     wglu_ref, bglu_ref, wo_ref) = refs[:N_MIX_REFS]
    yd_scr = refs[-1]
    y_a = (hf_ref[...] + hb_ref[...]) * _silu(ga_ref[...])
    z = _gelu_tanh(yf_ref[...] + yb_ref[...])
    glu = jax.nn.sigmoid(jnp.dot(z.astype(BF16), wglu_ref[0], preferred_element_type=F32)
                         + bglu_ref[0])
    y_s = z * glu * _silu(gs_ref[...])
    y_d = _batch_to_rows(yd_ref, yd_scr)
    out = (jnp.dot(y_a.astype(BF16), wo_ref[0, 0:W_LRU, :], preferred_element_type=F32)
           + jnp.dot(y_s.astype(BF16), wo_ref[0, W_LRU:W_LRU + W_S5, :], preferred_element_type=F32)
           + jnp.dot(y_d.astype(BF16), wo_ref[0, W_LRU + W_S5:, :], preferred_element_type=F32))
    rows, d = out.shape
    gate = mod_ref[0, 0, 2]
    hn = h_ref[...] + (out.reshape(rows // BATCH, BATCH, d) * gate[None]).reshape(rows, d)
    if final:
        fg_ref, o_ref, out_scr = refs[N_MIX_REFS:-1]
        hn = hn * lax.rsqrt(jnp.mean(hn * hn, axis=-1, keepdims=True) + EPS) * fg_ref[...]
        _rows_to_batch(hn, out_scr, 0, o_ref, 0, F32)
    else:
        n_proj_in = 8
        hs_ref = refs[N_MIX_REFS + n_proj_in]
        hs_ref[...] = hn
        _project(pl.program_id(0), hn, *refs[N_MIX_REFS:N_MIX_REFS + n_proj_in],
                 *refs[N_MIX_REFS + n_proj_in + 1:-1], ncb)


def _out_proj(hf, hb, yf, yb, pf, yd, hs, layer, mods, w_glu, b_glu, w_out, *, n_ctx,
              final_g=None, next_proj=None):
    final = final_g is not None
    n_rows, d = hs.shape
    tt = TOKEN_TILE
    tm = tt * BATCH
    ncb = n_ctx // tt
    off = ncb if final else 0
    nblk = n_rows // tm - off
    blk = lambda w, c=0: pl.BlockSpec((tm, w), lambda i: (i + off, c))
    const = lambda a: _layer_block(a, layer)
    kind = lambda i: (i + off >= ncb).astype(jnp.int32)
    in_specs = [blk(W_LRU), blk(W_LRU), blk(W_S5), blk(W_S5),
                blk(W_LRU, 2), blk(W_S5, 3),
                pl.BlockSpec((BATCH, tt, W_DA), lambda i: (0, i + off, 0)), blk(d),
                pl.BlockSpec((1, 1, 3, BATCH, d), lambda i: (layer, kind(i), 0, 0, 0)),
                const(w_glu), const(b_glu), const(w_out)]
    operands = [hf, hb, yf, yb, pf, pf, yd, hs, mods, w_glu, b_glu, w_out]
    assert len(operands) == N_MIX_REFS
    yd_scratch = pltpu.VMEM((W_DA // LANES, tm, LANES), F32)
    if final:
        final_g = final_g.reshape(1, d)
        in_specs += [pl.BlockSpec(final_g.shape, lambda i: (0, 0))]
        operands += [final_g]
        out_specs = pl.BlockSpec((BATCH, tt, d), lambda i: (0, i, 0))
        out_shape = jax.ShapeDtypeStruct((BATCH, nblk * tt, d), F32)
        scratch = [pltpu.VMEM((d // LANES, tm, LANES), F32), yd_scratch]
    else:
        p_in, p_out, p_shape, p_scratch = _project_specs(layer + 1, next_proj, tt, n_rows, kind)
        in_specs += p_in
        operands += [*next_proj[:4], *next_proj[4]]
        out_specs = [pl.BlockSpec((tm, d), lambda i: (i, 0))] + p_out
        out_shape = [jax.ShapeDtypeStruct((n_rows, d), F32)] + p_shape
        scratch = [p_scratch, yd_scratch]
    return pl.pallas_call(
        functools.partial(_out_proj_kernel, final=final, ncb=ncb),
        grid=(nblk,),
        in_specs=in_specs,
        out_specs=out_specs,
        out_shape=out_shape,
        scratch_shapes=scratch,
        compiler_params=_cparams(("arbitrary",)),
    )(*operands)


def _block_diag(w):
    nb, k, n = w.shape[-3:]
    rows = w.reshape(w.shape[:-3] + (nb * k, n))
    tiled = jnp.tile(rows, (1,) * (rows.ndim - 1) + (nb,))
    row_blk = lax.broadcasted_iota(jnp.int32, (nb * k, nb * n), 0) // k
    col_blk = lax.broadcasted_iota(jnp.int32, (nb * k, nb * n), 1) // n
    return jnp.where(row_blk == col_blk, tiled, jnp.zeros_like(tiled))


def kernel(x, c, ctx, c_ctx, norm_g, w_mod, b_mod, w_in, w_out, lru_conv_w, lru_conv_b, lru_wa, lru_ba,
           lru_wx, lru_bx, lru_lam, s5_lam_re, s5_lam_im, s5_log_dt, s5_b_re, s5_b_im, s5_c_re, s5_c_im,
           s5_d, s5_w_glu, s5_b_glu, da_lam, da_norm_g, final_g):
    bsz, t_len, d = x.shape
    n_ctx = ctx.shape[1]
    s_len = n_ctx + t_len
    n_rows = s_len * bsz
    n_layers = w_in.shape[0]
    assert bsz == BATCH and bsz + 1 <= MOD_ROWS
    assert n_ctx % Q_TILE == 0 and t_len % Q_TILE == 0
    assert n_ctx % TOKEN_TILE == 0 and n_ctx % SCAN_TILE == 0

    cc = jnp.concatenate([c, c_ctx[None], jnp.zeros((MOD_ROWS - bsz - 1, d), F32)], axis=0)
    mod = _modulation(cc, w_mod, b_mod).reshape(n_layers, MOD_ROWS, 3, d)
    mod_lat = mod[:, :bsz].transpose(0, 2, 1, 3)
    mod_ctx = jnp.broadcast_to(mod[:, bsz:bsz + 1], (n_layers, bsz, 3, d)).transpose(0, 2, 1, 3)
    mods = jnp.stack([mod_ctx, mod_lat], axis=1)

    tables = _rope_tables()

    side = W_LRU + W_S5 + 2 * W_DA
    c_ua, c_us = (0, W_LRU), (W_LRU, W_LRU + W_S5)
    c_k, c_v = (W_LRU + W_S5, W_LRU + W_S5 + W_DA), (W_LRU + W_S5 + W_DA, side)
    sl = lambda lo_hi, base=0: w_in[:, :, base + lo_hi[0]:base + lo_hi[1]]
    wf = jnp.concatenate([sl(c_ua), sl(c_us), sl(c_ua, side), sl(c_us, side)], axis=-1).astype(BF16)
    wa = jnp.concatenate([sl(c_k), sl(c_v), sl(c_k, side), sl(c_v, side)], axis=-1).astype(BF16)
    w_out_b = w_out.astype(BF16)
    w_glu_b = s5_w_glu.astype(BF16)

    bd = _block_diag
    wg = jnp.concatenate([bd(lru_wa), bd(lru_wx)], axis=-1).astype(BF16)
    bg = jnp.concatenate([lru_ba, lru_bx], axis=-1).reshape(n_layers, 2, 1, 2 * W_LRU)
    lru_lam_r = lru_lam.reshape(n_layers, 2, 1, W_LRU)
    lru_conv_b_r = lru_conv_b.reshape(n_layers, 1, W_LRU)
    flat = lambda a: a.reshape(n_layers, 2, 1, S5_N)
    ldt = jnp.broadcast_to(s5_log_dt[..., None], (n_layers, 2, S5_G, S5_P))
    bre = bd(jnp.swapaxes(s5_b_re, -1, -2))
    bim = bd(jnp.swapaxes(s5_b_im, -1, -2))
    cre = bd(jnp.swapaxes(s5_c_re, -1, -2)).astype(BF16)
    cim = bd(jnp.swapaxes(s5_c_im, -1, -2)).astype(BF16)
    lam_re, lam_im, ldt = flat(s5_lam_re), flat(s5_lam_im), flat(ldt)

    lru_params = (lru_conv_w, lru_conv_b_r, wg, bg, lru_lam_r)
    s5_params = (lam_re, lam_im, ldt, bre, bim, cre, cim, s5_d.reshape(n_layers, 1, W_S5))
    proj = (mods, norm_g.reshape(n_layers, 1, d), wf, wa, tables)
    da_lam_r = da_lam.reshape(n_layers, 4, DA_DH)
    da_g_r = da_norm_g.reshape(n_layers, 1, DA_DV)
    b_glu_r = s5_b_glu.reshape(n_layers, 1, W_S5)

    hs, pf, kq, v, gd = _in_proj(ctx, x, proj)
    for l in range(n_layers):
        final = l == n_layers - 1
        lam_init = 0.8 - 0.6 * math.exp(-0.3 * l)
        pf3 = pf.reshape(s_len, bsz, PF_W)

        hf, hb, yf, yb = _scans(pf3, l, lru_params, s5_params, n_ctx=n_ctx)

        yd = _attention(kq, v, gd, l, da_lam_r, da_g_r, n_ctx=n_ctx, lam_init=lam_init)

        mix = (hf.reshape(n_rows, W_LRU), hb.reshape(n_rows, W_LRU),
               yf.reshape(n_rows, W_S5), yb.reshape(n_rows, W_S5),
               pf, yd, hs, l, mods, w_glu_b, b_glu_r, w_out_b)
        if final:
            return _out_proj(*mix, n_ctx=n_ctx, final_g=final_g)
        hs, pf, kq, v, gd = _out_proj(*mix, n_ctx=n_ctx, next_proj=proj)
```

```python
import functools
import math

import jax
import jax.numpy as jnp
from jax import lax
from jax.experimental import pallas as pl
from jax.experimental.pallas import tpu as pltpu

F32 = jnp.float32
BF16 = jnp.bfloat16

GRID_W = 64
EPS = 1e-6
W_LRU = 256
LRU_BLOCKS = 4
CONV_W = 4
LRU_C = 8.0
W_S5 = 256
S5_H = 16
S5_G = 16
S5_P = 64
S5_N = S5_G * S5_P
DA_HEADS = 4
DA_DH = 64
DA_DV = 2 * DA_DH
W_DA = DA_HEADS * DA_DV
ROPE_BASE = 10000.0
ROPE_F = DA_DH // 4
LANES = 128
BATCH = 8

PF_W = 2 * W_LRU + 2 * W_S5
PA_W = 4 * W_DA

TOKEN_TILE = 64
PROJ_PARTS = 2
SCAN_TILE = 128
Q_TILE = 256
MOD_ROWS = 16
VMEM_LIMIT = 48 * 1024 * 1024


def _layer_block(a, layer):
    index = (layer,) + (0,) * (a.ndim - 1)
    return pl.BlockSpec((1,) + a.shape[1:], lambda *_: index, pipeline_mode=pl.Buffered(1))


def _cparams(sem):
    return pltpu.CompilerParams(dimension_semantics=sem, vmem_limit_bytes=VMEM_LIMIT)


def _silu(x):
    return x * jax.nn.sigmoid(x)


def _softplus(x):
    return jnp.maximum(x, 0.0) + jnp.log1p(jnp.exp(-jnp.abs(x)))


def _rows_to_batch(val, scr, slab0, out_ref, col0, dtype, t0=0):
    rows, width = val.shape
    tt = rows // BATCH
    r0 = t0 * BATCH
    for j in range(width // LANES):
        scr[slab0 + j, r0:r0 + rows, :] = val[:, j * LANES:(j + 1) * LANES]
    for b in range(BATCH):
        for j in range(width // LANES):
            out_ref[b, t0:t0 + tt, col0 + j * LANES:col0 + (j + 1) * LANES] = (
                scr[slab0 + j, pl.ds(r0 + b, tt, stride=BATCH), :].astype(dtype))


def _batch_to_rows(in_ref, scr):
    _, tt, width = in_ref.shape
    for b in range(BATCH):
        for j in range(width // LANES):
            scr[j, pl.ds(b, tt, stride=BATCH), :] = in_ref[b, :, j * LANES:(j + 1) * LANES].astype(F32)
    return jnp.concatenate([scr[j] for j in range(width // LANES)], axis=1)


def _rope_table_kernel(csp_ref, snp_ref, csr_ref, snr_ref):
    def table(shape, pos_shift):
        pos = lax.broadcasted_iota(jnp.int32, shape, 0) >> pos_shift
        lane = lax.broadcasted_iota(jnp.int32, shape, 1)
        f = (lane & (ROPE_F - 1)).astype(F32)
        ang = pos.astype(F32) * jnp.exp(-math.log(ROPE_BASE) * f / ROPE_F)
        sn = jnp.sin(ang)
        return jnp.cos(ang), jnp.where((lane & ROPE_F) == 0, -sn, sn)

    csp_ref[...], snp_ref[...] = table(csp_ref.shape, 0)
    csr_ref[...], snr_ref[...] = table(csr_ref.shape, int(math.log2(BATCH)))


def _rope_tables():
    shapes = [(GRID_W, LANES)] * 2 + [(GRID_W * BATCH, LANES)] * 2
    return pl.pallas_call(
        _rope_table_kernel,
        out_shape=[jax.ShapeDtypeStruct(s, F32) for s in shapes],
        compiler_params=pltpu.CompilerParams(vmem_limit_bytes=VMEM_LIMIT),
    )()


def _mod_kernel(c_ref, w_ref, b_ref, o_ref):
    c = c_ref[...]
    o_ref[0] = (jnp.dot(_silu(c).astype(BF16), w_ref[0].astype(BF16), preferred_element_type=F32)
                + b_ref[0])


def _modulation(cc, w_mod, b_mod):
    n_layers, d, d3 = w_mod.shape
    return pl.pallas_call(
        _mod_kernel,
        grid=(n_layers, d3 // d),
        in_specs=[pl.BlockSpec((MOD_ROWS, d), lambda l, j: (0, 0)),
                  pl.BlockSpec((1, d, d), lambda l, j: (l, 0, j)),
                  pl.BlockSpec((1, 1, d), lambda l, j: (l, 0, j))],
        out_specs=pl.BlockSpec((1, MOD_ROWS, d), lambda l, j: (l, 0, j)),
        out_shape=jax.ShapeDtypeStruct((n_layers, MOD_ROWS, d3), F32),
        compiler_params=_cparams(("arbitrary", "arbitrary")),
    )(cc, w_mod, b_mod.reshape(n_layers, 1, d3))


def _rope(x, cos, sin):
    width = x.shape[1]
    reps = width // LANES
    cos = jnp.concatenate([cos] * reps, axis=1)
    sin = jnp.concatenate([sin] * reps, axis=1)
    lane = lax.broadcasted_iota(jnp.int32, x.shape, 1)
    first = (lane & ROPE_F) == 0
    partner = jnp.where(first, pltpu.roll(x, width - ROPE_F, 1), pltpu.roll(x, ROPE_F, 1))
    return x * cos + partner * sin


def _project(i, x, mod_ref, g_ref, wf_ref, wa_ref, csp_ref, snp_ref, csr_ref, snr_ref,
             pf_ref, kq_ref, v_ref, gd_ref, scr, ncb):
    rows_all, d = x.shape
    rows = rows_all // PROJ_PARTS
    tt = rows // BATCH
    shift = mod_ref[0, 0, 0]
    scale = mod_ref[0, 0, 1]
    grid_row = jnp.maximum(i - ncb, 0)
    is_ctx = i < ncb
    lane = lax.broadcasted_iota(jnp.int32, (rows, LANES), 1)
    use_col = (lane & (DA_DH // 2)) != 0
    slabs = W_DA // LANES
    for part in range(PROJ_PARTS):
        r0 = part * rows
        xp = x[r0:r0 + rows]
        y = xp * lax.rsqrt(jnp.mean(xp * xp, axis=-1, keepdims=True) + EPS) * g_ref[0]
        y3 = y.reshape(tt, BATCH, d)
        n = (y3 * (1.0 + scale)[None] + shift[None]).reshape(rows, d).astype(BF16)
        pf_ref[r0:r0 + rows, :] = jnp.dot(n, wf_ref[0], preferred_element_type=F32)
        att = jnp.dot(n, wa_ref[0], preferred_element_type=F32)
        cos = jnp.where(use_col, csr_ref[r0:r0 + rows, :], csp_ref[pl.ds(grid_row, 1), :])
        sin = jnp.where(use_col, snr_ref[r0:r0 + rows, :], snp_ref[pl.ds(grid_row, 1), :])
        cos = jnp.where(is_ctx, 1.0, cos)
        sin = jnp.where(is_ctx, 0.0, sin)
        k = _rope(att[:, :W_DA], cos, sin)
        v = att[:, W_DA:2 * W_DA]
        q = _rope(att[:, 2 * W_DA:3 * W_DA], cos, sin) * (DA_DH ** -0.5 * math.log2(math.e))
        _rows_to_batch(k, scr, 0, kq_ref, 0, BF16, part * tt)
        _rows_to_batch(q, scr, slabs, kq_ref, W_DA, BF16, part * tt)
        _rows_to_batch(v, scr, 2 * slabs, v_ref, 0, BF16, part * tt)
        _rows_to_batch(att[:, 3 * W_DA:], scr, 3 * slabs, gd_ref, 0, F32, part * tt)


def _in_proj_kernel(ctx_ref, x_ref, *refs, ncb):
    proj_in, (hs_ref, pf_ref, kq_ref, v_ref, gd_ref, scr, rows_scr) = refs[:8], refs[8:]
    i = pl.program_id(0)

    @pl.when(i < ncb)
    def _():
        hs_ref[...] = _batch_to_rows(ctx_ref, rows_scr)

    @pl.when(i >= ncb)
    def _():
        hs_ref[...] = _batch_to_rows(x_ref, rows_scr)

    _project(i, hs_ref[...], *proj_in, pf_ref, kq_ref, v_ref, gd_ref, scr, ncb)


def _project_specs(layer, proj, tt, n_rows, kind):
    mods, g, wf, wa, tables = proj
    d = wf.shape[1]
    tm = tt * BATCH
    s_len = n_rows // BATCH
    const = lambda shape: pl.BlockSpec(shape, lambda i: (0,) * len(shape),
                                       pipeline_mode=pl.Buffered(1))
    in_specs = [pl.BlockSpec((1, 1, 3, BATCH, d), lambda i: (layer, kind(i), 0, 0, 0)),
                _layer_block(g, layer), _layer_block(wf, layer), _layer_block(wa, layer)
                ] + [const(t.shape) for t in tables]
    out_specs = [pl.BlockSpec((tm, PF_W), lambda i: (i, 0)),
                 pl.BlockSpec((BATCH, tt, 2 * W_DA), lambda i: (0, i, 0)),
                 pl.BlockSpec((BATCH, tt, W_DA), lambda i: (0, i, 0)),
                 pl.BlockSpec((BATCH, tt, W_DA), lambda i: (0, i, 0))]
    out_shape = [jax.ShapeDtypeStruct((n_rows, PF_W), F32),
                 jax.ShapeDtypeStruct((BATCH, s_len, 2 * W_DA), BF16),
                 jax.ShapeDtypeStruct((BATCH, s_len, W_DA), BF16),
                 jax.ShapeDtypeStruct((BATCH, s_len, W_DA), F32)]
    scratch = pltpu.VMEM((PA_W // LANES, tm, LANES), F32)
    return in_specs, out_specs, out_shape, scratch


def _in_proj(ctx, x, proj):
    bsz, n_ctx, d = ctx.shape
    tt = TOKEN_TILE
    assert tt == GRID_W
    tm = tt * BATCH
    ncb = n_ctx // tt
    n = ncb + x.shape[1] // tt
    kind = lambda i: (i >= ncb).astype(jnp.int32)
    in_specs, out_specs, out_shape, scratch = _project_specs(0, proj, tt, n * tm, kind)
    return pl.pallas_call(
        functools.partial(_in_proj_kernel, ncb=ncb),
        grid=(n,),
        in_specs=[pl.BlockSpec((bsz, tt, d), lambda i: (0, jnp.minimum(i, ncb - 1), 0)),
                  pl.BlockSpec((bsz, tt, d), lambda i: (0, jnp.maximum(i - ncb, 0), 0))] + in_specs,
        out_specs=[pl.BlockSpec((tm, d), lambda i: (i, 0))] + out_specs,
        out_shape=[jax.ShapeDtypeStruct((n * tm, d), F32)] + out_shape,
        scratch_shapes=[scratch, pltpu.VMEM((d // LANES, tm, LANES), F32)],
        compiler_params=_cparams(("arbitrary",)),
    )(ctx, x, *proj[:4], *proj[4])


def _bwd_chunk(i, ncc, n):
    return jnp.where(i < ncc, ncc - 1 - i, n - 1 - i + ncc)


S5_LANE_CHUNK = 512
N_LRU_IN, N_S5_IN = 11, 10


def _lru_gates(d, c, u_c, u_p, u_n, cw_ref, cb_ref, wg_ref, bg_ref, lam_ref, a_scr, b_scr, ts, ncc, n):
    seg_start = jnp.logical_or(c == 0, c == ncc)
    seg_end = jnp.logical_or(c == ncc - 1, c == n - 1)
    prev = u_p[...] * jnp.where(seg_start, 0.0, 1.0)
    nxt = u_n[...] * jnp.where(seg_end, 0.0, 1.0)
    u = jnp.concatenate([prev, u_c[...], nxt], axis=0)
    cw = cw_ref[0]
    xc = cb_ref[0] + cw[0:1] * u[0:ts]
    for j in range(1, CONV_W):
        xc = xc + cw[j:j + 1] * u[j:j + ts]
    xc2 = xc.reshape(ts * xc.shape[1], W_LRU)
    gts = jnp.dot(xc2.astype(BF16), wg_ref[0, d], preferred_element_type=F32) + bg_ref[0, d]
    gate_r = jax.nn.sigmoid(gts[:, :W_LRU])
    gate_i = jax.nn.sigmoid(gts[:, W_LRU:])
    log_a = -LRU_C * gate_r * _softplus(-lam_ref[0, d])
    a = jnp.exp(log_a)
    th = jnp.tanh(log_a)
    one_minus_a2 = -2.0 * th / (1.0 - th)
    bb = jnp.sqrt(one_minus_a2) * (gate_i * xc2)
    a_scr[d] = a.reshape(xc.shape)
    b_scr[d] = bb.reshape(xc.shape)


def _s5_discretise(lre_ref, lim_ref, ldt_ref, bre_ref, bim_ref, bbar_scr, a_scr):
    bsz, nst = a_scr.shape[2:]
    for d in range(2):
        lre = lre_ref[0, d]
        lim = lim_ref[0, d]
        dt = jnp.exp(ldt_ref[0, d])
        mag = jnp.exp(lre * dt)
        ang = lim * dt
        ar = mag * jnp.cos(ang)
        ai = mag * jnp.sin(ang)
        nr = ar - 1.0
        den = lre * lre + lim * lim
        cfr = (nr * lre + ai * lim) / den
        cfi = (ai * lre - nr * lim) / den
        bre = bre_ref[0, d]
        bim = bim_ref[0, d]
        bbar_scr[d, :, :nst] = (cfr * bre - cfi * bim).astype(BF16)
        bbar_scr[d, :, nst:] = (cfr * bim + cfi * bre).astype(BF16)
        a_scr[d, 0] = jnp.broadcast_to(ar, (bsz, nst))
        a_scr[d, 1] = jnp.broadcast_to(ai, (bsz, nst))


def _scan_kernel(*refs, ts, ncc, n):
    lru_in, s5_in = refs[:N_LRU_IN], refs[N_LRU_IN:N_LRU_IN + N_S5_IN]
    hf_ref, hb_ref, yf_ref, yb_ref = refs[N_LRU_IN + N_S5_IN:N_LRU_IN + N_S5_IN + 4]
    la_scr, lb_scr, lcarry_scr, bbar_scr, sa_scr, buf_scr, scarry_scr = refs[N_LRU_IN + N_S5_IN + 4:]
    ufc, ufp, ufn, ubc, ubp, ubn, cw_ref, cb_ref, wg_ref, bg_ref, lam_ref = lru_in
    uf_ref, ub_ref, lre_ref, lim_ref, ldt_ref, bre_ref, bim_ref, cre_ref, cim_ref, dsk_ref = s5_in
    i = pl.program_id(0)
    nst = S5_N
    bsz = uf_ref.shape[1]

    @pl.when(i == 0)
    def _():
        lcarry_scr[...] = jnp.zeros_like(lcarry_scr)
        scarry_scr[...] = jnp.zeros_like(scarry_scr)
        _s5_discretise(lre_ref, lim_ref, ldt_ref, bre_ref, bim_ref, bbar_scr, sa_scr)

    for d, u_ref in enumerate((uf_ref, ub_ref)):
        u2 = u_ref[...].reshape(ts * bsz, W_S5).astype(BF16)
        bu = jnp.dot(u2, bbar_scr[d], preferred_element_type=F32)
        buf_scr[d] = bu.reshape(ts, bsz, 2 * nst)

    lru_par = (cw_ref, cb_ref, wg_ref, bg_ref, lam_ref, la_scr, lb_scr, ts, ncc, n)
    _lru_gates(0, i, ufc, ufp, ufn, *lru_par)
    _lru_gates(1, _bwd_chunk(i, ncc, n), ubc, ubp, ubn, *lru_par)
    hf = lcarry_scr[0]
    hb = lcarry_scr[1]
    for k in range(ts):
        kb = ts - 1 - k
        hf = la_scr[0, k] * hf + lb_scr[0, k]
        hb = la_scr[1, kb] * hb + lb_scr[1, kb]
        hf_ref[k] = hf
        hb_ref[kb] = hb
    lcarry_scr[0] = hf
    lcarry_scr[1] = hb

    lc = S5_LANE_CHUNK
    chains = [(d, pl.ds(c0, lc), pl.ds(nst + c0, lc)) for d in range(2) for c0 in range(0, nst, lc)]
    state = [(scarry_scr[d, :, re_sl], scarry_scr[d, :, im_sl]) for d, re_sl, im_sl in chains]
    for k in range(ts):
        for n_chain, (d, re_sl, im_sl) in enumerate(chains):
            t = k if d == 0 else ts - 1 - k
            ar = sa_scr[d, 0, :, re_sl]
            ai = sa_scr[d, 1, :, re_sl]
            sre, sim = state[n_chain]
            nre = ar * sre - ai * sim + buf_scr[d, t, :, re_sl]
            nim = ar * sim + ai * sre + buf_scr[d, t, :, im_sl]
            buf_scr[d, t, :, re_sl] = nre
            buf_scr[d, t, :, im_sl] = nim
            state[n_chain] = (nre, nim)
    for (d, re_sl, im_sl), (sre, sim) in zip(chains, state):
        scarry_scr[d, :, re_sl] = sre
        scarry_scr[d, :, im_sl] = sim

    for d, y_ref in enumerate((yf_ref, yb_ref)):
        st = buf_scr[d].reshape(ts * bsz, 2 * nst)
        y = (jnp.dot(st[:, :nst].astype(BF16), cre_ref[0, d], preferred_element_type=F32)
             - jnp.dot(st[:, nst:].astype(BF16), cim_ref[0, d], preferred_element_type=F32))
        y = y.reshape(ts, bsz, W_S5)
        y_ref[...] = dsk_ref[0] * uf_ref[...] + y if d == 0 else y


def _scans(pf3, layer, lru_params, s5_params, *, n_ctx):
    s_len, bsz, _ = pf3.shape
    ts = SCAN_TILE
    n = s_len // ts
    ncc = n_ctx // ts
    fwd = lambda i: i
    bwd = lambda i: _bwd_chunk(i, ncc, n)
    s5_col = W_LRU // W_S5

    def cur(cfn, col=0):
        return pl.BlockSpec((ts, bsz, W_LRU), lambda i: (cfn(i), 0, col))

    def prev(cfn):
        return pl.BlockSpec((2, bsz, W_LRU), lambda i: (jnp.maximum(cfn(i) * (ts // 2) - 1, 0), 0, 0))

    def nxt(cfn):
        return pl.BlockSpec((1, bsz, W_LRU), lambda i: (jnp.minimum((cfn(i) + 1) * ts, s_len - 1), 0, 0))

    const = lambda a: _layer_block(a, layer)
    assert len(lru_params) == N_LRU_IN - 6 and len(s5_params) == N_S5_IN - 2
    out_block = lambda cfn: pl.BlockSpec((ts, bsz, W_LRU), lambda i: (cfn(i), 0, 0))
    return pl.pallas_call(
        functools.partial(_scan_kernel, ts=ts, ncc=ncc, n=n),
        grid=(n,),
        in_specs=([cur(fwd), prev(fwd), nxt(fwd), cur(bwd), prev(bwd), nxt(bwd)]
                  + [const(a) for a in lru_params]
                  + [cur(fwd, s5_col), cur(bwd, s5_col)] + [const(a) for a in s5_params]),
        out_specs=[out_block(fwd), out_block(bwd), out_block(fwd), out_block(bwd)],
        out_shape=[jax.ShapeDtypeStruct((s_len, bsz, W_LRU), F32)] * 4,
        scratch_shapes=[pltpu.VMEM((2, ts, bsz, W_LRU), F32),
                        pltpu.VMEM((2, ts, bsz, W_LRU), F32),
                        pltpu.VMEM((2, bsz, W_LRU), F32),
                        pltpu.VMEM((2, W_S5, 2 * S5_N), BF16),
                        pltpu.VMEM((2, 2, bsz, S5_N), F32),
                        pltpu.VMEM((2, ts, bsz, 2 * S5_N), F32),
                        pltpu.VMEM((2, bsz, 2 * S5_N), F32)],
        compiler_params=_cparams(("arbitrary",)),
    )(*([pf3] * 6), *lru_params, pf3, pf3, *s5_params)


V_ROWS = DA_DV + 16
SCORE_PARTS = 2
SCORE_SLOTS = 4


def _attn_kernel(q_ref, k_ref, v_ref, gd_ref, dl_ref, g_ref, o_ref, s_scr, vt_scr, *, tq, n_ctx,
                 lam_init):
    s_len = k_ref.shape[1]
    nq = s_len // tq
    ncq = n_ctx // tq
    for c0 in range(0, s_len, tq):
        vt_scr[0:DA_DV, c0:c0 + tq] = v_ref[0, c0:c0 + tq, :].astype(F32).T.astype(BF16)
    pad_row = lax.broadcasted_iota(jnp.int32, (V_ROWS - DA_DV, s_len), 0)
    vt_scr[DA_DV:, :] = jnp.where(pad_row == 0, 1.0, 0.0).astype(BF16)
    dl = dl_ref[0]
    lam = (jnp.exp(jnp.sum(dl[0:1] * dl[1:2], axis=-1, keepdims=True))
           - jnp.exp(jnp.sum(dl[2:3] * dl[3:4], axis=-1, keepdims=True)) + lam_init)
    gscale = g_ref[0] * (1.0 - lam_init)
    lane = lax.broadcasted_iota(jnp.int32, (tq, LANES), 1)

    def rows(qt):
        return pl.ds(pl.multiple_of(qt * tq, tq), tq)

    def key_parts(nk, max_parts):
        groups = nk // LANES
        n_parts = min(max_parts, groups)
        bounds = [LANES * ((groups * p) // n_parts) for p in range(n_parts + 1)]
        return tuple(zip(bounds[:-1], bounds[1:]))

    def scores(qt, mp, nk, slot):
        q = q_ref[0, rows(qt), :]
        qm = jnp.where((lane < DA_DH) if mp == 0 else (lane >= DA_DH), q, jnp.zeros_like(q))
        m = None
        for lo, hi in key_parts(nk, SCORE_PARTS):
            s_t = lax.dot_general(k_ref[0, lo:hi, :], qm, (((1,), (1,)), ((), ())),
                                  preferred_element_type=F32)
            s_scr[slot, lo:hi, :] = s_t
            m_part = jnp.max(s_t, axis=0, keepdims=True)
            m = m_part if m is None else jnp.maximum(m, m_part)
        return m

    def values(slot, m, nk):
        p_t = jnp.exp2(s_scr[slot, 0:nk, :] - m).astype(BF16)
        return jnp.dot(vt_scr[:, 0:nk], p_t, preferred_element_type=F32)

    def finish(qt, o1, o2):
        o_t = (o1[:DA_DV] * (1.0 / o1[DA_DV:DA_DV + 1])
               - o2[:DA_DV] * (lam / o2[DA_DV:DA_DV + 1]))
        y_t = o_t * lax.rsqrt(jnp.mean(o_t * o_t, axis=0, keepdims=True) + EPS)
        y = y_t.T * gscale * _silu(gd_ref[0, rows(qt), :])
        o_ref[0, rows(qt), :] = y.astype(o_ref.dtype)

    for qt in range(ncq):
        m0 = scores(qt, 0, n_ctx, 0)
        m1 = scores(qt, 1, n_ctx, 2)
        finish(qt, values(0, m0, n_ctx), values(2, m1, n_ctx))

    def pair(t0, m0, last):
        m1 = scores(t0, 1, s_len, 2)
        o1 = values(0, m0, s_len)
        m0b = scores(t0 + 1, 0, s_len, 1)
        o2 = values(2, m1, s_len)
        finish(t0, o1, o2)
        m1b = scores(t0 + 1, 1, s_len, 3)
        o1b = values(1, m0b, s_len)
        m0_next = None if last else scores(t0 + 2, 0, s_len, 0)
        o2b = values(3, m1b, s_len)
        finish(t0 + 1, o1b, o2b)
        return m0_next

    n_pairs = (nq - ncq) // 2
    m0 = lax.fori_loop(0, n_pairs - 1, lambda j, m0: pair(ncq + 2 * j, m0, False),
                       scores(ncq, 0, s_len, 0))
    pair(nq - 2, m0, True)


def _attention(qk, v, gd, layer, da_lam, da_g, *, n_ctx, lam_init):
    bsz, s_len, _ = qk.shape
    tq = Q_TILE
    assert n_ctx % tq == 0 and (s_len - n_ctx) % (2 * tq) == 0
    return pl.pallas_call(
        functools.partial(_attn_kernel, tq=tq, n_ctx=n_ctx, lam_init=lam_init),
        grid=(bsz, DA_HEADS),
        in_specs=[pl.BlockSpec((1, s_len, LANES), lambda b, h: (b, 0, DA_HEADS + h)),
                  pl.BlockSpec((1, s_len, LANES), lambda b, h: (b, 0, h)),
                  pl.BlockSpec((1, s_len, LANES), lambda b, h: (b, 0, h)),
                  pl.BlockSpec((1, s_len, LANES), lambda b, h: (b, 0, h)),
                  _layer_block(da_lam, layer), _layer_block(da_g, layer)],
        out_specs=pl.BlockSpec((1, s_len, LANES), lambda b, h: (b, 0, h)),
        out_shape=jax.ShapeDtypeStruct((bsz, s_len, W_DA), BF16),
        scratch_shapes=[pltpu.VMEM((SCORE_SLOTS, s_len, tq), F32),
                        pltpu.VMEM((V_ROWS, s_len), BF16)],
        compiler_params=_cparams(("arbitrary", "arbitrary")),
    )(qk, qk, v, gd, da_lam, da_g)


def _gelu_tanh(x):
    return 0.5 * x * (1.0 + jnp.tanh(math.sqrt(2.0 / math.pi) * (x + 0.044715 * (x * x * x))))


N_MIX_REFS = 12


def _out_proj_kernel(*refs, final, ncb):
    (hf_ref, hb_ref, yf_ref, yb_ref, ga_ref, gs_ref, yd_ref, h_ref, mod_ref,
     wglu_ref, bglu_ref, wo_ref) = refs[:N_MIX_REFS]
    yd_scr = refs[-1]
    y_a = (hf_ref[...] + hb_ref[...]) * _silu(ga_ref[...])
    z = _gelu_tanh(yf_ref[...] + yb_ref[...])
    glu = jax.nn.sigmoid(jnp.dot(z.astype(BF16), wglu_ref[0], preferred_element_type=F32)
                         + bglu_ref[0])
    y_s = z * glu * _silu(gs_ref[...])
    y_d = _batch_to_rows(yd_ref, yd_scr)
    out = (jnp.dot(y_a.astype(BF16), wo_ref[0, 0:W_LRU, :], preferred_element_type=F32)
           + jnp.dot(y_s.astype(BF16), wo_ref[0, W_LRU:W_LRU + W_S5, :], preferred_element_type=F32)
           + jnp.dot(y_d.astype(BF16), wo_ref[0, W_LRU + W_S5:, :], preferred_element_type=F32))
    rows, d = out.shape
    gate = mod_ref[0, 0, 2]
    hn = h_ref[...] + (out.reshape(rows // BATCH, BATCH, d) * gate[None]).reshape(rows, d)
    if final:
        fg_ref, o_ref, out_scr = refs[N_MIX_REFS:-1]
        hn = hn * lax.rsqrt(jnp.mean(hn * hn, axis=-1, keepdims=True) + EPS) * fg_ref[...]
        _rows_to_batch(hn, out_scr, 0, o_ref, 0, F32)
    else:
        n_proj_in = 8
        hs_ref = refs[N_MIX_REFS + n_proj_in]
        hs_ref[...] = hn
        _project(pl.program_id(0), hn, *refs[N_MIX_REFS:N_MIX_REFS + n_proj_in],
                 *refs[N_MIX_REFS + n_proj_in + 1:-1], ncb)


def _out_proj(hf, hb, yf, yb, pf, yd, hs, layer, mods, w_glu, b_glu, w_out, *, n_ctx,
              final_g=None, next_proj=None):
    final = final_g is not None
    n_rows, d = hs.shape
    tt = TOKEN_TILE
    tm = tt * BATCH
    ncb = n_ctx // tt
    off = ncb if final else 0
    nblk = n_rows // tm - off
    blk = lambda w, c=0: pl.BlockSpec((tm, w), lambda i: (i + off, c))
    const = lambda a: _layer_block(a, layer)
    kind = lambda i: (i + off >= ncb).astype(jnp.int32)
    in_specs = [blk(W_LRU), blk(W_LRU), blk(W_S5), blk(W_S5),
                blk(W_LRU, 2), blk(W_S5, 3),
                pl.BlockSpec((BATCH, tt, W_DA), lambda i: (0, i + off, 0)), blk(d),
                pl.BlockSpec((1, 1, 3, BATCH, d), lambda i: (layer, kind(i), 0, 0, 0)),
                const(w_glu), const(b_glu), const(w_out)]
    operands = [hf, hb, yf, yb, pf, pf, yd, hs, mods, w_glu, b_glu, w_out]
    assert len(operands) == N_MIX_REFS
    yd_scratch = pltpu.VMEM((W_DA // LANES, tm, LANES), F32)
    if final:
        final_g = final_g.reshape(1, d)
        in_specs += [pl.BlockSpec(final_g.shape, lambda i: (0, 0))]
        operands += [final_g]
        out_specs = pl.BlockSpec((BATCH, tt, d), lambda i: (0, i, 0))
        out_shape = jax.ShapeDtypeStruct((BATCH, nblk * tt, d), F32)
        scratch = [pltpu.VMEM((d // LANES, tm, LANES), F32), yd_scratch]
    else:
        p_in, p_out, p_shape, p_scratch = _project_specs(layer + 1, next_proj, tt, n_rows, kind)
        in_specs += p_in
        operands += [*next_proj[:4], *next_proj[4]]
        out_specs = [pl.BlockSpec((tm, d), lambda i: (i, 0))] + p_out
        out_shape = [jax.ShapeDtypeStruct((n_rows, d), F32)] + p_shape
        scratch = [p_scratch, yd_scratch]
    return pl.pallas_call(
        functools.partial(_out_proj_kernel, final=final, ncb=ncb),
        grid=(nblk,),
        in_specs=in_specs,
        out_specs=out_specs,
        out_shape=out_shape,
        scratch_shapes=scratch,
        compiler_params=_cparams(("arbitrary",)),
    )(*operands)


def _block_diag(w):
    nb, k, n = w.shape[-3:]
    rows = w.reshape(w.shape[:-3] + (nb * k, n))
    tiled = jnp.tile(rows, (1,) * (rows.ndim - 1) + (nb,))
    row_blk = lax.broadcasted_iota(jnp.int32, (nb * k, nb * n), 0) // k
    col_blk = lax.broadcasted_iota(jnp.int32, (nb * k, nb * n), 1) // n
    return jnp.where(row_blk == col_blk, tiled, jnp.zeros_like(tiled))


def kernel(x, c, ctx, c_ctx, norm_g, w_mod, b_mod, w_in, w_out, lru_conv_w, lru_conv_b, lru_wa, lru_ba,
           lru_wx, lru_bx, lru_lam, s5_lam_re, s5_lam_im, s5_log_dt, s5_b_re, s5_b_im, s5_c_re, s5_c_im,
           s5_d, s5_w_glu, s5_b_glu, da_lam, da_norm_g, final_g):
    bsz, t_len, d = x.shape
    n_ctx = ctx.shape[1]
    s_len = n_ctx + t_len
    n_rows = s_len * bsz
    n_layers = w_in.shape[0]
    assert bsz == BATCH and bsz + 1 <= MOD_ROWS
    assert n_ctx % Q_TILE == 0 and t_len % Q_TILE == 0
    assert n_ctx % TOKEN_TILE == 0 and n_ctx % SCAN_TILE == 0

    cc = jnp.concatenate([c, c_ctx[None], jnp.zeros((MOD_ROWS - bsz - 1, d), F32)], axis=0)
    mod = _modulation(cc, w_mod, b_mod).reshape(n_layers, MOD_ROWS, 3, d)
    mod_lat = mod[:, :bsz].transpose(0, 2, 1, 3)
    mod_ctx = jnp.broadcast_to(mod[:, bsz:bsz + 1], (n_layers, bsz, 3, d)).transpose(0, 2, 1, 3)
    mods = jnp.stack([mod_ctx, mod_lat], axis=1)

    tables = _rope_tables()

    side = W_LRU + W_S5 + 2 * W_DA
    c_ua, c_us = (0, W_LRU), (W_LRU, W_LRU + W_S5)
    c_k, c_v = (W_LRU + W_S5, W_LRU + W_S5 + W_DA), (W_LRU + W_S5 + W_DA, side)
    sl = lambda lo_hi, base=0: w_in[:, :, base + lo_hi[0]:base + lo_hi[1]]
    wf = jnp.concatenate([sl(c_ua), sl(c_us), sl(c_ua, side), sl(c_us, side)], axis=-1).astype(BF16)
    wa = jnp.concatenate([sl(c_k), sl(c_v), sl(c_k, side), sl(c_v, side)], axis=-1).astype(BF16)
    w_out_b = w_out.astype(BF16)
    w_glu_b = s5_w_glu.astype(BF16)

    bd = _block_diag
    wg = jnp.concatenate([bd(lru_wa), bd(lru_wx)], axis=-1).astype(BF16)
    bg = jnp.concatenate([lru_ba, lru_bx], axis=-1).reshape(n_layers, 2, 1, 2 * W_LRU)
    lru_lam_r = lru_lam.reshape(n_layers, 2, 1, W_LRU)
    lru_conv_b_r = lru_conv_b.reshape(n_layers, 1, W_LRU)
    flat = lambda a: a.reshape(n_layers, 2, 1, S5_N)
    ldt = jnp.broadcast_to(s5_log_dt[..., None], (n_layers, 2, S5_G, S5_P))
    bre = bd(jnp.swapaxes(s5_b_re, -1, -2))
    bim = bd(jnp.swapaxes(s5_b_im, -1, -2))
    cre = bd(jnp.swapaxes(s5_c_re, -1, -2)).astype(BF16)
    cim = bd(jnp.swapaxes(s5_c_im, -1, -2)).astype(BF16)
    lam_re, lam_im, ldt = flat(s5_lam_re), flat(s5_lam_im), flat(ldt)

    lru_params = (lru_conv_w, lru_conv_b_r, wg, bg, lru_lam_r)
    s5_params = (lam_re, lam_im, ldt, bre, bim, cre, cim, s5_d.reshape(n_layers, 1, W_S5))
    proj = (mods, norm_g.reshape(n_layers, 1, d), wf, wa, tables)
    da_lam_r = da_lam.reshape(n_layers, 4, DA_DH)
    da_g_r = da_norm_g.reshape(n_layers, 1, DA_DV)
    b_glu_r = s5_b_glu.reshape(n_layers, 1, W_S5)

    hs, pf, kq, v, gd = _in_proj(ctx, x, proj)
    for l in range(n_layers):
        final = l == n_layers - 1
        lam_init = 0.8 - 0.6 * math.exp(-0.3 * l)
        pf3 = pf.reshape(s_len, bsz, PF_W)

        hf, hb, yf, yb = _scans(pf3, l, lru_params, s5_params, n_ctx=n_ctx)

        yd = _attention(kq, v, gd, l, da_lam_r, da_g_r, n_ctx=n_ctx, lam_init=lam_init)

        mix = (hf.reshape(n_rows, W_LRU), hb.reshape(n_rows, W_LRU),
               yf.reshape(n_rows, W_S5), yb.reshape(n_rows, W_S5),
               pf, yd, hs, l, mods, w_glu_b, b_glu_r, w_out_b)
        if final:
            return _out_proj(*mix, n_ctx=n_ctx, final_g=final_g)
        hs, pf, kq, v, gd = _out_proj(*mix, n_ctx=n_ctx, next_proj=proj)
```
